```python
import math
import jax, jax.numpy as jnp
from jax import lax
import numpy as np

D_MODEL = 2048
BATCH = 8
SEQ = 2048
DEPTH = 2

N_A_LAYERS = DEPTH // 2
N_B_LAYERS = DEPTH - N_A_LAYERS
PLE_DIM = 256
NORM_EPS = 1e-6
SSM_GROUP = 16
SSM_GROUPS = D_MODEL // SSM_GROUP
SSM_STATE = 64
SSM_CHUNK = 128
DT_MIN = 0.001
DT_MAX = 0.1
HEAD_DIM = 128
N_Q_HEADS = D_MODEL // HEAD_DIM
N_KV_HEADS = 8
KV_REP = N_Q_HEADS // N_KV_HEADS
DILATED_PATTERNS = ((128, 1), (512, 4), (2048, 16))
N_PATTERNS = len(DILATED_PATTERNS)
ATTN_BLOCK = 128
ROPE_THETA = 10000.0
NEG_INF = -1e30
N_GROUPS = 4
EXPERTS_PER_GROUP = 8
N_EXPERTS = N_GROUPS * EXPERTS_PER_GROUP
TOP_K_IN_GROUP = 2
EXPERT_HIDDEN = D_MODEL // 4

kernel_name = "yoco_s5_dilated_hmoe_trunk"

F32 = jnp.float32


def _rmsnorm(a, g):
    af = a.astype(F32)
    r = lax.rsqrt(jnp.mean(af * af, axis=-1, keepdims=True) + NORM_EPS)
    return (af * r).astype(a.dtype) * g


def _rope_tables(length):
    inv = 1.0 / (ROPE_THETA ** (jnp.arange(0, HEAD_DIM, 2, dtype=F32) / HEAD_DIM))
    ang = jnp.arange(length, dtype=F32)[:, None] * inv[None, :]
    return jnp.cos(ang), jnp.sin(ang)


def _rope(a, cos, sin):
    a1, a2 = jnp.split(a.astype(F32), 2, axis=-1)
    c = cos[None, :, None, :]
    s = sin[None, :, None, :]
    return jnp.concatenate([a1 * c - a2 * s, a2 * c + a1 * s], axis=-1).astype(a.dtype)


def _ssm_combine(left, right):
    a_l, b_l = left
    a_r, b_r = right
    return a_r * a_l, a_r * b_l + b_r


def _s5_mixer(u, a_re, a_im, log_dt, b_re, b_im, c_re, c_im, d_skip, w_glu, b_glu):
    bsz, length, _ = u.shape
    n_chunks = length // SSM_CHUNK
    uf = u.astype(F32).reshape(bsz, n_chunks, SSM_CHUNK, SSM_GROUPS, SSM_GROUP)
    uf = uf.transpose(1, 0, 2, 3, 4)
    lam = lax.complex(a_re.astype(F32), a_im.astype(F32))
    dt = jnp.exp(log_dt.astype(F32))[:, None]
    lam_bar = jnp.exp(lam * dt)
    b_mat = lax.complex(b_re.astype(F32), b_im.astype(F32))
    b_bar = ((lam_bar - 1.0) / lam)[..., None] * b_mat
    c_mat = lax.complex(c_re.astype(F32), c_im.astype(F32))
    a_chunk = jnp.broadcast_to(lam_bar, (SSM_CHUNK, SSM_GROUPS, SSM_STATE))[None]

    def chunk_step(h_prev, u_c):
        bu = jnp.einsum('btgc,gpc->btgp', u_c.astype(jnp.complex64), b_bar)
        a_cum, s = lax.associative_scan(_ssm_combine, (a_chunk, bu), axis=1)
        s = s + a_cum * h_prev[:, None]
        y = jnp.einsum('btgp,gcp->btgc', s, c_mat).real
        return s[:, -1], y

    h0 = jnp.zeros((bsz, SSM_GROUPS, SSM_STATE), jnp.complex64)
    _, ys = lax.scan(chunk_step, h0, uf)
    y = ys.transpose(1, 0, 2, 3, 4).reshape(bsz, length, D_MODEL)
    y = y + d_skip.astype(F32) * u.astype(F32)
    g = jax.nn.gelu(y).astype(u.dtype)
    z = g @ w_glu + b_glu
    val, gate = jnp.split(z, 2, axis=-1)
    return (val.astype(F32) * jax.nn.sigmoid(gate.astype(F32))).astype(u.dtype)


def _dilated_window_attention(q, k, v, dilation, n_steps):
    bsz, length, _, _ = q.shape
    m_len = length // dilation
    nblk = -(-m_len // ATTN_BLOCK)
    m_pad = nblk * ATTN_BLOCK

    def strided(a):
        return a.reshape(bsz, m_len, dilation, a.shape[2], HEAD_DIM).transpose(0, 2, 1, 3, 4)

    qs = jnp.pad(strided(q), ((0, 0), (0, 0), (0, m_pad - m_len), (0, 0), (0, 0)))
    qs = qs.reshape(bsz, dilation, nblk, ATTN_BLOCK, N_KV_HEADS, KV_REP, HEAD_DIM)

    def key_blocks(a):
        a = jnp.pad(strided(a), ((0, 0), (0, 0), (ATTN_BLOCK, m_pad - m_len), (0, 0), (0, 0)))
        a = a.reshape(bsz, dilation, nblk + 1, ATTN_BLOCK, N_KV_HEADS, HEAD_DIM)
        return jnp.concatenate([a[:, :, :-1], a[:, :, 1:]], axis=3)

    kb = key_blocks(k)
    vb = key_blocks(v)
    scale = HEAD_DIM ** -0.5
    scores = jnp.einsum('brnqhgd,brnshd->brnhgqs', qs, kb, preferred_element_type=F32) * scale
    qi = jnp.arange(ATTN_BLOCK)[:, None]
    si = jnp.arange(2 * ATTN_BLOCK)[None, :]
    dist = ATTN_BLOCK + qi - si
    band = (dist >= 0) & (dist <= n_steps)
    key_pos = (jnp.arange(nblk)[:, None] - 1) * ATTN_BLOCK + jnp.arange(2 * ATTN_BLOCK)[None, :]
    mask = band[None] & (key_pos >= 0)[:, None, :]
    scores = jnp.where(mask[:, None, None], scores, NEG_INF)
    m = jnp.max(scores, axis=-1, keepdims=True)
    e = jnp.exp(scores - m)
    den = jnp.sum(e, axis=-1, keepdims=True)
    o = jnp.einsum('brnhgqs,brnshd->brnhgqd', e, vb.astype(F32)) / den
    lse = (m + jnp.log(den))[..., 0]
    o = o.transpose(0, 1, 2, 5, 3, 4, 6).reshape(bsz, dilation, m_pad, N_Q_HEADS, HEAD_DIM)[:, :, :m_len]
    o = o.transpose(0, 2, 1, 3, 4).reshape(bsz, length, N_Q_HEADS, HEAD_DIM)
    lse = lse.transpose(0, 1, 2, 5, 3, 4).reshape(bsz, dilation, m_pad, N_Q_HEADS)[:, :, :m_len]
    lse = lse.transpose(0, 2, 1, 3).reshape(bsz, length, N_Q_HEADS)
    return o, lse


def _dilated_mixer(u, w_q, w_o, k, v, cos, sin):
    bsz, length, _ = u.shape
    q = (u @ w_q).reshape(bsz, length, N_PATTERNS * N_Q_HEADS, HEAD_DIM)
    q = _rope(q, cos, sin).reshape(bsz, length, N_PATTERNS, N_Q_HEADS, HEAD_DIM)
    outs, lses = [], []
    for g, (window, dilation) in enumerate(DILATED_PATTERNS):
        o, lse = _dilated_window_attention(q[:, :, g], k, v, dilation, window // dilation)
        outs.append(o)
        lses.append(lse)
    wts = jax.nn.softmax(jnp.stack(lses, axis=0), axis=0)
    o = jnp.einsum('pblh,pblhd->blhd', wts, jnp.stack(outs, axis=0))
    return o.reshape(bsz, length, N_Q_HEADS * HEAD_DIM).astype(u.dtype) @ w_o


def _hier_moe(u, r_grp_w, r_grp_b, r_exp_w, r_exp_b, w_gate, w_up, w_down):
    bsz, length, _ = u.shape
    t = u.reshape(-1, D_MODEL)
    n_tok = t.shape[0]
    grp_logits = (t @ r_grp_w + r_grp_b).astype(F32)
    grp_prob = jax.nn.softmax(grp_logits, axis=-1)
    grp_idx = jnp.argmax(grp_logits, axis=-1)
    grp_w = jnp.take_along_axis(grp_prob, grp_idx[:, None], axis=-1)
    exp_logits = (t @ r_exp_w + r_exp_b).astype(F32).reshape(n_tok, N_GROUPS, EXPERTS_PER_GROUP)
    sel = jnp.take_along_axis(exp_logits, grp_idx[:, None, None], axis=1)[:, 0]
    top_v, top_i = lax.top_k(sel, TOP_K_IN_GROUP)
    top_w = jax.nn.softmax(top_v, axis=-1) * grp_w
    expert_id = grp_idx[:, None] * EXPERTS_PER_GROUP + top_i
    gates = jnp.einsum('tk,tke->te', top_w, jax.nn.one_hot(expert_id, N_EXPERTS, dtype=F32))

    def expert_step(acc, args):
        wg, wu, wd, gate_e = args
        hid = jax.nn.silu(t @ wg) * (t @ wu)
        return acc + gate_e[:, None].astype(t.dtype) * (hid @ wd), None

    acc, _ = lax.scan(expert_step, jnp.zeros_like(t), (w_gate, w_up, w_down, gates.T))
    return acc.reshape(bsz, length, D_MODEL)


def _per_layer_embedding(h, p_i, norm_g, w_gate, w_proj):
    gate = jax.nn.sigmoid((_rmsnorm(h, norm_g) @ w_gate).astype(F32))
    return (gate * (p_i @ w_proj).astype(F32)).astype(h.dtype)


def setup_inputs(seed: int = 0) -> dict:
    key = jax.random.key(seed)
    ks = iter(jax.random.split(key, 40))

    def nrm(shape, scale):
        return jax.random.normal(next(ks), shape, F32) * scale

    def gain(shape):
        return 1.0 + nrm(shape, 0.01)

    na, nb, dep = N_A_LAYERS, N_B_LAYERS, DEPTH
    g, pst, gc = SSM_GROUPS, SSM_STATE, SSM_GROUP
    qw = N_PATTERNS * N_Q_HEADS * HEAD_DIM
    kvw = N_KV_HEADS * HEAD_DIM
    return {
        "x": nrm((BATCH, SEQ, D_MODEL), 1.0),
        "p": nrm((DEPTH, BATCH, SEQ, PLE_DIM), 1.0),
        "s5_norm_g": gain((na, D_MODEL)),
        "s5_a_re": -0.5 + nrm((na, g, pst), 0.01),
        "s5_a_im": math.pi * jnp.arange(pst, dtype=F32)[None, None, :] + nrm((na, g, pst), 0.01),
        "s5_log_dt": jax.random.uniform(next(ks), (na, g), F32, math.log(DT_MIN), math.log(DT_MAX)),
        "s5_b_re": nrm((na, g, pst, gc), (2.0 * gc) ** -0.5),
        "s5_b_im": nrm((na, g, pst, gc), (2.0 * gc) ** -0.5),
        "s5_c_re": nrm((na, g, gc, pst), pst ** -0.5),
        "s5_c_im": nrm((na, g, gc, pst), pst ** -0.5),
        "s5_d": nrm((na, D_MODEL), 1.0),
        "s5_w_glu": nrm((na, D_MODEL, 2 * D_MODEL), D_MODEL ** -0.5),
        "s5_b_glu": nrm((na, 2 * D_MODEL), 0.01),
        "kv_norm_g": gain((D_MODEL,)),
        "w_k": nrm((D_MODEL, kvw), D_MODEL ** -0.5),
        "w_v": nrm((D_MODEL, kvw), D_MODEL ** -0.5),
        "attn_norm_g": gain((nb, D_MODEL)),
        "w_q": nrm((nb, D_MODEL, qw), D_MODEL ** -0.5),
        "w_o": nrm((nb, N_Q_HEADS * HEAD_DIM, D_MODEL), (N_Q_HEADS * HEAD_DIM) ** -0.5),
        "ffn_norm_g": gain((dep, D_MODEL)),
        "router_grp_w": nrm((dep, D_MODEL, N_GROUPS), D_MODEL ** -0.5),
        "router_grp_b": nrm((dep, N_GROUPS), 0.01),
        "router_exp_w": nrm((dep, D_MODEL, N_EXPERTS), D_MODEL ** -0.5),
        "router_exp_b": nrm((dep, N_EXPERTS), 0.01),
        "w_gate": nrm((dep, N_EXPERTS, D_MODEL, EXPERT_HIDDEN), D_MODEL ** -0.5),
        "w_up": nrm((dep, N_EXPERTS, D_MODEL, EXPERT_HIDDEN), D_MODEL ** -0.5),
        "w_down": nrm((dep, N_EXPERTS, EXPERT_HIDDEN, D_MODEL), EXPERT_HIDDEN ** -0.5),
        "ple_norm_g": gain((dep, D_MODEL)),
        "ple_gate_w": nrm((dep, D_MODEL, D_MODEL), D_MODEL ** -0.5),
        "ple_proj_w": nrm((dep, PLE_DIM, D_MODEL), PLE_DIM ** -0.5),
        "final_norm_g": gain((D_MODEL,)),
    }


def reference(x, p, s5_norm_g, s5_a_re, s5_a_im, s5_log_dt, s5_b_re, s5_b_im, s5_c_re, s5_c_im,
              s5_d, s5_w_glu, s5_b_glu, kv_norm_g, w_k, w_v, attn_norm_g, w_q, w_o,
              ffn_norm_g, router_grp_w, router_grp_b, router_exp_w, router_exp_b,
              w_gate, w_up, w_down, ple_norm_g, ple_gate_w, ple_proj_w, final_norm_g):
    bsz, length, _ = x.shape
    cos, sin = _rope_tables(length)
    h = x
    k_shared = None
    v_shared = None
    for i in range(DEPTH):
        if i < N_A_LAYERS:
            h = h + _s5_mixer(_rmsnorm(h, s5_norm_g[i]), s5_a_re[i], s5_a_im[i], s5_log_dt[i],
                              s5_b_re[i], s5_b_im[i], s5_c_re[i], s5_c_im[i], s5_d[i],
                              s5_w_glu[i], s5_b_glu[i])
        else:
            j = i - N_A_LAYERS
            if j == 0:
                kv_in = _rmsnorm(h, kv_norm_g)
                k_shared = _rope((kv_in @ w_k).reshape(bsz, length, N_KV_HEADS, HEAD_DIM), cos, sin)
                v_shared = (kv_in @ w_v).reshape(bsz, length, N_KV_HEADS, HEAD_DIM)
            h = h + _dilated_mixer(_rmsnorm(h, attn_norm_g[j]), w_q[j], w_o[j],
                                   k_shared, v_shared, cos, sin)
        h = h + _hier_moe(_rmsnorm(h, ffn_norm_g[i]), router_grp_w[i], router_grp_b[i],
                          router_exp_w[i], router_exp_b[i], w_gate[i], w_up[i], w_down[i])
        h = h + _per_layer_embedding(h, p[i], ple_norm_g[i], ple_gate_w[i], ple_proj_w[i])
    return _rmsnorm(h, final_norm_g)
```

```python
import functools
import math

import jax
import jax.numpy as jnp
from jax import lax
from jax.experimental import pallas as pl
from jax.experimental.pallas import tpu as pltpu

F32 = jnp.float32
BF16 = jnp.bfloat16
I32 = jnp.int32

NORM_EPS = 1e-6
LANES = 128
SUBLANES = 8
MXU_DIM = 256
VMEM_LIMIT_BYTES = 56 * 1024 * 1024

SSM_GROUP = 16
SSM_STATE = 64
SSM_GROUP_BLOCK = MXU_DIM // SSM_GROUP
SSM_CHUNK = 128
HEAD_DIM = 128
N_Q_HEADS = 16
N_KV_HEADS = 8
KV_REP = N_Q_HEADS // N_KV_HEADS
DILATED_PATTERNS = ((128, 1), (512, 4), (2048, 16))
ATTN_BLOCK = 128
ROPE_THETA = 10000.0
NEG_INF = -1e30
N_GROUPS = 4
EXPERTS_PER_GROUP = 8
N_EXPERTS = N_GROUPS * EXPERTS_PER_GROUP
MOE_TILE = 256
ROUTER_TILE = 256


def _params(sem):
    return pltpu.CompilerParams(dimension_semantics=sem, vmem_limit_bytes=VMEM_LIMIT_BYTES)


def _rms_scale(x):
    return lax.rsqrt(jnp.mean(x * x, axis=-1, keepdims=True) + NORM_EPS)


def _rmsnorm_kernel(x_ref, g_ref, o_ref):
    x = x_ref[...]
    o_ref[...] = (x * _rms_scale(x) * g_ref[...]).astype(o_ref.dtype)


def rmsnorm(x, g, out_dtype=F32, tm=512):
    t, d = x.shape
    return pl.pallas_call(
        _rmsnorm_kernel,
        grid=(t // tm,),
        in_specs=[pl.BlockSpec((tm, d), lambda i: (i, 0)),
                  pl.BlockSpec((1, d), lambda i: (0, 0))],
        out_specs=pl.BlockSpec((tm, d), lambda i: (i, 0)),
        out_shape=jax.ShapeDtypeStruct((t, d), out_dtype),
        compiler_params=_params(("parallel",)),
        name="rmsnorm",
    )(x, g.reshape(1, d))


def _s5_discretize_kernel(are_ref, aim_ref, ldt_ref, bre_ref, bim_ref,
                          lre_ref, lim_ref, bbre_ref, bbim_ref):
    a_re, a_im = are_ref[...], aim_ref[...]
    dt = jnp.exp(ldt_ref[...])
    mag = jnp.exp(dt * a_re)
    l_re = mag * jnp.cos(dt * a_im)
    l_im = mag * jnp.sin(dt * a_im)
    n_re, n_im = l_re - 1.0, l_im
    den = a_re * a_re + a_im * a_im
    f_re = (n_re * a_re + n_im * a_im) / den
    f_im = (n_im * a_re - n_re * a_im) / den
    b_re, b_im = bre_ref[...], bim_ref[...]
    lre_ref[...] = l_re
    lim_ref[...] = l_im
    bbre_ref[...] = f_re * b_re - f_im * b_im
    bbim_ref[...] = f_re * b_im + f_im * b_re


def _block_diag(m):
    nb, gb, a, b = m.shape
    eye = jnp.eye(gb, dtype=m.dtype)
    return jnp.einsum("ngab,gh->ngahb", m, eye).reshape(nb, gb * a, gb * b)


def s5_prepare(a_re, a_im, log_dt, b_re, b_im, c_re, c_im):
    g, p = a_re.shape
    gc = b_re.shape[-1]
    rep = lambda v: jnp.repeat(v, gc, axis=-1)
    shp = jax.ShapeDtypeStruct((g, p * gc), F32)
    l_re, l_im, bb_re, bb_im = pl.pallas_call(
        _s5_discretize_kernel, out_shape=(shp, shp, shp, shp), name="s5_discretize",
    )(rep(a_re), rep(a_im), rep(jnp.broadcast_to(log_dt[:, None], (g, p))),
      b_re.reshape(g, p * gc), b_im.reshape(g, p * gc))
    nb = g // SSM_GROUP_BLOCK
    lam_re = l_re[:, ::gc].reshape(nb, 1, SSM_GROUP_BLOCK * p)
    lam_im = l_im[:, ::gc].reshape(nb, 1, SSM_GROUP_BLOCK * p)

    def in_mat(bb):
        m = bb.reshape(g, p, gc).transpose(0, 2, 1).reshape(nb, SSM_GROUP_BLOCK, gc, p)
        return _block_diag(m)

    def out_mat(c):
        m = c.transpose(0, 2, 1).reshape(nb, SSM_GROUP_BLOCK, p, gc)
        return _block_diag(m)

    w_in = jnp.concatenate([in_mat(bb_re), in_mat(bb_im)], axis=-1).astype(BF16)
    w_out = jnp.concatenate([out_mat(c_re), out_mat(-c_im)], axis=1).astype(BF16)
    return lam_re, lam_im, w_in, w_out


def _s5_kernel(u_ref, win_ref, lre_ref, lim_ref, wout_ref, d_ref, o_ref,
               lhs_scr, st_scr, bu_scr, y_scr, *, nb, ch, half):
    k = pl.program_id(1)
    rows = nb * ch
    nslab = lhs_scr.shape[0]

    @pl.when(k == 0)
    def _():
        st_scr[...] = jnp.zeros_like(st_scr)

    for b in range(nb):
        for s in range(nslab):
            lhs_scr[s, pl.ds(b, ch, stride=nb), :] = u_ref[b, :, s * LANES:(s + 1) * LANES]
    u_tm = jnp.concatenate([lhs_scr[s] for s in range(nslab)], axis=1)
    bu_scr[...] = jnp.dot(u_tm.astype(BF16), win_ref[0], preferred_element_type=F32)

    l_re = jnp.broadcast_to(lre_ref[0], (nb, half))
    l_im = jnp.broadcast_to(lim_ref[0], (nb, half))

    def step(t, carry):
        s_re, s_im = carry
        r0 = pl.multiple_of(t * nb, nb)
        n_re = l_re * s_re - l_im * s_im + bu_scr[pl.ds(r0, nb), 0:half]
        n_im = l_re * s_im + l_im * s_re + bu_scr[pl.ds(r0, nb), half:2 * half]
        bu_scr[pl.ds(r0, nb), 0:half] = n_re
        bu_scr[pl.ds(r0, nb), half:2 * half] = n_im
        return n_re, n_im

    s_re, s_im = lax.fori_loop(0, ch, step, (st_scr[:, 0:half], st_scr[:, half:2 * half]), unroll=2)
    st_scr[:, 0:half] = s_re
    st_scr[:, half:2 * half] = s_im

    y = jnp.dot(bu_scr[...].astype(BF16), wout_ref[0], preferred_element_type=F32)
    y = jax.nn.gelu(y + d_ref[...] * u_tm)
    for s in range(nslab):
        y_scr[s] = y[:, s * LANES:(s + 1) * LANES]
    for b in range(nb):
        for s in range(nslab):
            o_ref[b, :, s * LANES:(s + 1) * LANES] = (
                y_scr[s, pl.ds(b, ch, stride=nb), :].astype(o_ref.dtype))


def s5_core(u3, lam_re, lam_im, w_in, w_out, d_skip):
    nb, length, d = u3.shape
    assert nb == SUBLANES, "the scan keeps one batch row per sublane"
    cb = w_in.shape[1]
    half = w_in.shape[2] // 2
    ch = SSM_CHUNK
    kern = functools.partial(_s5_kernel, nb=nb, ch=ch, half=half)
    return pl.pallas_call(
        kern,
        grid=(d // cb, length // ch),
        in_specs=[
            pl.BlockSpec((nb, ch, cb), lambda g, k: (0, k, g)),
            pl.BlockSpec((1, cb, 2 * half), lambda g, k: (g, 0, 0)),
            pl.BlockSpec((1, 1, half), lambda g, k: (g, 0, 0)),
            pl.BlockSpec((1, 1, half), lambda g, k: (g, 0, 0)),
            pl.BlockSpec((1, 2 * half, cb), lambda g, k: (g, 0, 0)),
            pl.BlockSpec((1, cb), lambda g, k: (0, g)),
        ],
        out_specs=pl.BlockSpec((nb, ch, cb), lambda g, k: (0, k, g)),
        out_shape=jax.ShapeDtypeStruct((nb, length, d), BF16),
        scratch_shapes=[
            pltpu.VMEM((cb // LANES, nb * ch, LANES), F32),
            pltpu.VMEM((nb, 2 * half), F32),
            pltpu.VMEM((nb * ch, 2 * half), F32),
            pltpu.VMEM((cb // LANES, nb * ch, LANES), F32),
        ],
        compiler_params=_params(("parallel", "arbitrary")),
        name="s5_core",
    )(u3, w_in, lam_re, lam_im, w_out, d_skip.reshape(1, d))


def _glu_kernel(g_ref, wv_ref, wg_ref, bv_ref, bg_ref, x_ref, o_ref):
    a = g_ref[...]
    val = jnp.dot(a, wv_ref[...], preferred_element_type=F32) + bv_ref[...]
    gate = jnp.dot(a, wg_ref[...], preferred_element_type=F32) + bg_ref[...]
    o_ref[...] = x_ref[...] + val * jax.nn.sigmoid(gate)


def glu_residual(g, w_glu, b_glu, x, tm=1024, tn=512):
    t, d = x.shape
    nj = d // tn
    b2 = b_glu.reshape(1, 2 * d)
    return pl.pallas_call(
        _glu_kernel,
        grid=(t // tm, nj),
        in_specs=[
            pl.BlockSpec((tm, d), lambda i, j: (i, 0)),
            pl.BlockSpec((d, tn), lambda i, j: (0, j)),
            pl.BlockSpec((d, tn), lambda i, j: (0, j + nj)),
            pl.BlockSpec((1, tn), lambda i, j: (0, j)),
            pl.BlockSpec((1, tn), lambda i, j: (0, j + nj)),
            pl.BlockSpec((tm, tn), lambda i, j: (i, j)),
        ],
        out_specs=pl.BlockSpec((tm, tn), lambda i, j: (i, j)),
        out_shape=jax.ShapeDtypeStruct((t, d), F32),
        compiler_params=_params(("parallel", "arbitrary")),
        name="glu_residual",
    )(g, w_glu, w_glu, b2, b2, x)


def _router_kernel(h_ref, g_ref, wr_ref, br_ref, u_ref, meta_ref, cnt_ref, base_scr, *, tm):
    i = pl.program_id(0)

    @pl.when(i == 0)
    def _():
        base_scr[...] = jnp.zeros_like(base_scr)

    x = h_ref[...]
    u = x * _rms_scale(x) * g_ref[...]
    u_ref[...] = u
    logits = jnp.dot(u, wr_ref[...], preferred_element_type=F32,
                     precision=lax.Precision.HIGHEST) + br_ref[...]
    lane = lax.broadcasted_iota(I32, logits.shape, 1)
    big = jnp.int32(1 << 20)
    is_grp = (lane >= N_EXPERTS) & (lane < N_EXPERTS + N_GROUPS)
    gl = jnp.where(is_grp, logits, -jnp.inf)
    gmax = jnp.max(gl, axis=-1, keepdims=True)
    gidx = jnp.min(jnp.where(gl == gmax, lane, big), axis=-1, keepdims=True) - N_EXPERTS
    gsum = jnp.sum(jnp.where(is_grp, jnp.exp(logits - gmax), 0.0), axis=-1, keepdims=True)
    grp_w = 1.0 / gsum
    in_grp = (lane < N_EXPERTS) & (jnp.right_shift(lane, EXPERTS_PER_GROUP.bit_length() - 1) == gidx)
    sel = jnp.where(in_grp, logits, -jnp.inf)
    v1 = jnp.max(sel, axis=-1, keepdims=True)
    i1 = jnp.min(jnp.where(sel == v1, lane, big), axis=-1, keepdims=True)
    sel2 = jnp.where(lane == i1, -jnp.inf, sel)
    v2 = jnp.max(sel2, axis=-1, keepdims=True)
    i2 = jnp.min(jnp.where(sel2 == v2, lane, big), axis=-1, keepdims=True)
    e21 = jnp.exp(v2 - v1)
    w1 = grp_w / (1.0 + e21)
    w2 = grp_w * e21 / (1.0 + e21)
    oh1 = lane == i1
    oh2 = lane == i2
    oh = (oh1 | oh2).astype(F32)
    rr = lax.broadcasted_iota(I32, (tm, tm), 0)
    cc = lax.broadcasted_iota(I32, (tm, tm), 1)
    tri = (cc < rr).astype(BF16)
    before = jnp.dot(tri, oh.astype(BF16), preferred_element_type=F32) + base_scr[...]
    r1 = jnp.sum(jnp.where(oh1, before, 0.0), axis=-1, keepdims=True)
    r2 = jnp.sum(jnp.where(oh2, before, 0.0), axis=-1, keepdims=True)
    base_scr[...] = base_scr[...] + jnp.sum(oh, axis=0, keepdims=True)
    cnt_ref[...] = base_scr[...]
    meta = jnp.where(lane == 0, i1.astype(F32), 0.0)
    meta = jnp.where(lane == 1, i2.astype(F32), meta)
    meta = jnp.where(lane == 2, w1, meta)
    meta = jnp.where(lane == 3, w2, meta)
    meta = jnp.where(lane == 4, r1, meta)
    meta = jnp.where(lane == 5, r2, meta)
    meta_ref[...] = meta


def moe_router(h, norm_g, r_grp_w, r_grp_b, r_exp_w, r_exp_b):
    t, d = h.shape
    tm = ROUTER_TILE
    pad = LANES - N_EXPERTS - N_GROUPS
    wr = jnp.concatenate([r_exp_w, r_grp_w, jnp.zeros((d, pad), F32)], axis=1)
    br = jnp.concatenate([r_exp_b, r_grp_b, jnp.zeros((pad,), F32)]).reshape(1, LANES)
    return pl.pallas_call(
        functools.partial(_router_kernel, tm=tm),
        grid=(t // tm,),
        in_specs=[
            pl.BlockSpec((tm, d), lambda i: (i, 0)),
            pl.BlockSpec((1, d), lambda i: (0, 0)),
            pl.BlockSpec((d, LANES), lambda i: (0, 0)),
            pl.BlockSpec((1, LANES), lambda i: (0, 0)),
        ],
        out_specs=[
            pl.BlockSpec((tm, d), lambda i: (i, 0)),
            pl.BlockSpec((tm, LANES), lambda i: (i, 0)),
            pl.BlockSpec((1, LANES), lambda i: (0, 0)),
        ],
        out_shape=[
            jax.ShapeDtypeStruct((t, d), F32),
            jax.ShapeDtypeStruct((t, LANES), F32),
            jax.ShapeDtypeStruct((1, LANES), F32),
        ],
        scratch_shapes=[pltpu.VMEM((1, LANES), F32)],
        compiler_params=_params(("arbitrary",)),
        name="moe_router",
    )(h, norm_g.reshape(1, d), wr, br)


def moe_plan(meta, counts, t):
    tile = MOE_TILE
    n_tiles = (2 * t) // tile + N_EXPERTS
    cnt = counts[0, :N_EXPERTS].astype(I32)
    padded = ((cnt + tile - 1) // tile) * tile
    ends = jnp.cumsum(padded)
    offs = ends - padded
    e = meta[:, 0:2].astype(I32)
    rank = meta[:, 4:6].astype(I32)
    dest = offs[e] + rank
    tok = jnp.broadcast_to(jnp.arange(t, dtype=I32)[:, None], (t, 2))
    tok_of_slot = jnp.zeros((n_tiles * tile,), I32).at[dest.reshape(-1)].set(tok.reshape(-1))
    starts = jnp.arange(n_tiles, dtype=I32) * tile
    tile_expert = jnp.minimum(
        jnp.sum((starts[:, None] >= ends[None, :]).astype(I32), axis=1), N_EXPERTS - 1)
    n_used = (ends[-1] // tile).astype(I32).reshape(1)
    return dest, tok_of_slot.reshape(n_tiles, 1, tile), tile_expert, n_used


def _moe_expert_kernel(te_ref, nu_ref, tok_ref, u_hbm, wg_ref, wu_ref, wd_ref, o_ref,
                       idx_smem, rows, wg_s, wu_s, wd_s, sem_i, sem_r, *, tile):
    i = pl.program_id(0)

    def row_copy(tok, r):
        return pltpu.make_async_copy(u_hbm.at[pl.ds(tok, 1)], rows.at[pl.ds(r, 1)], sem_r)

    @pl.when(i < nu_ref[0])
    def _():
        cp = pltpu.make_async_copy(tok_ref.at[0, 0], idx_smem, sem_i)
        cp.start()
        cp.wait()

        def issue(r, c):
            row_copy(idx_smem[r], r).start()
            return c

        lax.fori_loop(0, tile, issue, 0)

        prev = te_ref[jnp.maximum(i - 1, 0)]

        @pl.when((i == 0) | (te_ref[i] != prev))
        def _():
            wg_s[...] = wg_ref[0].astype(BF16)
            wu_s[...] = wu_ref[0].astype(BF16)
            wd_s[...] = wd_ref[0].astype(BF16)

        def drain(r, c):
            row_copy(0, r).wait()
            return c

        lax.fori_loop(0, tile, drain, 0)

        x = rows[...].astype(BF16)
        gate = jnp.dot(x, wg_s[...], preferred_element_type=F32)
        up = jnp.dot(x, wu_s[...], preferred_element_type=F32)
        hid = (jax.nn.silu(gate) * up).astype(BF16)
        o_ref[...] = jnp.dot(hid, wd_s[...], preferred_element_type=F32)

    @pl.when(i >= nu_ref[0])
    def _():
        o_ref[...] = jnp.zeros_like(o_ref)


def moe_experts(u, tok_of_slot, tile_expert, n_used, w_gate, w_up, w_down):
    t, d = u.shape
    n_tiles, _, tile = tok_of_slot.shape
    hdim = w_gate.shape[-1]
    grid_spec = pltpu.PrefetchScalarGridSpec(
        num_scalar_prefetch=2,
        grid=(n_tiles,),
        in_specs=[
            pl.BlockSpec((1, 1, tile), lambda i, te, nu: (i, 0, 0)),
            pl.BlockSpec(memory_space=pl.ANY),
            pl.BlockSpec((1, d, hdim), lambda i, te, nu: (te[i], 0, 0)),
            pl.BlockSpec((1, d, hdim), lambda i, te, nu: (te[i], 0, 0)),
            pl.BlockSpec((1, hdim, d), lambda i, te, nu: (te[i], 0, 0)),
        ],
        out_specs=pl.BlockSpec((tile, d), lambda i, te, nu: (i, 0)),
        scratch_shapes=[
            pltpu.SMEM((tile,), I32),
            pltpu.VMEM((tile, d), F32),
            pltpu.VMEM((d, hdim), BF16),
            pltpu.VMEM((d, hdim), BF16),
            pltpu.VMEM((hdim, d), BF16),
            pltpu.SemaphoreType.DMA,
            pltpu.SemaphoreType.DMA,
        ],
    )
    return pl.pallas_call(
        functools.partial(_moe_expert_kernel, tile=tile),
        grid_spec=grid_spec,
        out_shape=jax.ShapeDtypeStruct((n_tiles * tile, d), F32),
        compiler_params=_params(("arbitrary",)),
        name="moe_experts",
    )(tile_expert, n_used, tok_of_slot, u, w_gate, w_up, w_down)


def _combine_ple_kernel(dest_ref, ys_hbm, h_ref, meta_ref, p_ref, g_ref, wgate_ref, wproj_ref, fg_ref,
                        o_ref, idx_smem, rows, sem_i, sem_r, *, tm, final_norm):
    def row_copy(slot, k, r):
        return pltpu.make_async_copy(ys_hbm.at[pl.ds(slot, 1)], rows.at[k, pl.ds(r, 1)], sem_r)

    cp = pltpu.make_async_copy(dest_ref.at[0, 0], idx_smem, sem_i)
    cp.start()
    cp.wait()

    def issue(r, c):
        row_copy(idx_smem[2 * r], 0, r).start()
        row_copy(idx_smem[2 * r + 1], 1, r).start()
        return c

    lax.fori_loop(0, tm, issue, 0)

    def drain(r, c):
        row_copy(0, 0, r).wait()
        row_copy(0, 1, r).wait()
        return c

    lax.fori_loop(0, tm, drain, 0)

    meta = meta_ref[...]
    h2 = h_ref[...] + (meta[:, 2:3] * rows[0] + meta[:, 3:4] * rows[1])
    a = (h2 * _rms_scale(h2) * g_ref[...]).astype(BF16)
    gate = jax.nn.sigmoid(jnp.dot(a, wgate_ref[...], preferred_element_type=F32))
    proj = jnp.dot(p_ref[...].astype(BF16), wproj_ref[...], preferred_element_type=F32)
    h3 = h2 + gate * proj
    if final_norm:
        h3 = h3 * _rms_scale(h3) * fg_ref[...]
    o_ref[...] = h3


def moe_combine_ple(dest, ys, h, meta, p_i, ple_norm_g, ple_gate_w, ple_proj_w, final_g, final_norm, tm=256):
    t, d = h.shape
    pd = p_i.shape[-1]
    dest3 = dest.reshape(t // tm, 1, 2 * tm)
    return pl.pallas_call(
        functools.partial(_combine_ple_kernel, tm=tm, final_norm=final_norm),
        grid=(t // tm,),
        in_specs=[
            pl.BlockSpec((1, 1, 2 * tm), lambda i: (i, 0, 0)),
            pl.BlockSpec(memory_space=pl.ANY),
            pl.BlockSpec((tm, d), lambda i: (i, 0)),
            pl.BlockSpec((tm, LANES), lambda i: (i, 0)),
            pl.BlockSpec((tm, pd), lambda i: (i, 0)),
            pl.BlockSpec((1, d), lambda i: (0, 0)),
            pl.BlockSpec((d, d), lambda i: (0, 0)),
            pl.BlockSpec((pd, d), lambda i: (0, 0)),
            pl.BlockSpec((1, d), lambda i: (0, 0)),
        ],
        out_specs=pl.BlockSpec((tm, d), lambda i: (i, 0)),
        out_shape=jax.ShapeDtypeStruct((t, d), F32),
        scratch_shapes=[
            pltpu.SMEM((2 * tm,), I32),
            pltpu.VMEM((2, tm, d), F32),
            pltpu.SemaphoreType.DMA,
            pltpu.SemaphoreType.DMA,
        ],
        compiler_params=_params(("arbitrary",)),
        name="moe_combine_ple",
    )(dest3, ys, h, meta, p_i, ple_norm_g.reshape(1, d), ple_gate_w, ple_proj_w, final_g.reshape(1, d))


def moe_ple_layer(h, p_i, ffn_norm_g, r_grp_w, r_grp_b, r_exp_w, r_exp_b, w_gate, w_up, w_down,
                  ple_norm_g, ple_gate_w, ple_proj_w, final_g, final_norm):
    t = h.shape[0]
    u, meta, counts = moe_router(h, ffn_norm_g, r_grp_w, r_grp_b, r_exp_w, r_exp_b)
    dest, tok_of_slot, tile_expert, n_used = moe_plan(meta, counts, t)
    ys = moe_experts(u, tok_of_slot, tile_expert, n_used, w_gate, w_up, w_down)
    return moe_combine_ple(dest, ys, h, meta, p_i, ple_norm_g, ple_gate_w, ple_proj_w, final_g, final_norm)


def _qkv_kernel(x_ref, gq_ref, gkv_ref, w_ref, cos_ref, sin_ref, o_ref, act_scr, *, nq_tiles, scale):
    j = pl.program_id(1)

    @pl.when(j == 0)
    def _():
        x = x_ref[...]
        xn = x * _rms_scale(x)
        act_scr[0] = (xn * gq_ref[...]).astype(BF16)
        act_scr[1] = (xn * gkv_ref[...]).astype(BF16)

    a = act_scr[(j >= nq_tiles).astype(I32)]
    acc = jnp.dot(a, w_ref[...], preferred_element_type=F32)
    tn = acc.shape[1]
    reps = tn // HEAD_DIM
    cos = jnp.tile(cos_ref[...], (1, reps))
    sin = jnp.tile(sin_ref[...], (1, reps))
    lane = lax.broadcasted_iota(I32, acc.shape, 1)
    first_half = (lane & (HEAD_DIM - 1)) < (HEAD_DIM // 2)
    partner = jnp.where(first_half, pltpu.roll(acc, tn - HEAD_DIM // 2, 1), pltpu.roll(acc, HEAD_DIM // 2, 1))
    roped = acc * cos + partner * sin
    out = jnp.where(j <= nq_tiles, roped, acc)
    out = out * jnp.where(j < nq_tiles, scale, 1.0)
    o_ref[...] = out.astype(o_ref.dtype)


def qkv_project(h, attn_g, kv_g, w_all, cos_f, sin_f, length, tm=512, tn=1024):
    t, d = h.shape
    n_all = w_all.shape[1]
    kvw = N_KV_HEADS * HEAD_DIM
    assert tn == kvw
    nq_tiles = (n_all - 2 * kvw) // tn
    nl = length // tm
    return pl.pallas_call(
        functools.partial(_qkv_kernel, nq_tiles=nq_tiles, scale=HEAD_DIM ** -0.5),
        grid=(t // tm, n_all // tn),
        in_specs=[
            pl.BlockSpec((tm, d), lambda i, j: (i, 0)),
            pl.BlockSpec((1, d), lambda i, j: (0, 0)),
            pl.BlockSpec((1, d), lambda i, j: (0, 0)),
            pl.BlockSpec((d, tn), lambda i, j: (0, j)),
            pl.BlockSpec((tm, HEAD_DIM), lambda i, j: (i % nl, 0)),
            pl.BlockSpec((tm, HEAD_DIM), lambda i, j: (i % nl, 0)),
        ],
        out_specs=pl.BlockSpec((tm, tn), lambda i, j: (i, j)),
        out_shape=jax.ShapeDtypeStruct((t, n_all), BF16),
        scratch_shapes=[pltpu.VMEM((2, tm, d), BF16)],
        compiler_params=_params(("parallel", "arbitrary")),
        name="qkv_project",
    )(h, attn_g.reshape(1, d), kv_g.reshape(1, d), w_all, cos_f, sin_f)


def _attn_kernel(q0_ref, q1_ref, q2_ref, k_ref, v_ref, o_ref, q_scr, k_scr, v_scr, acc_scr, m_scr, l_scr,
                 *, length, patterns):
    blk = ATTN_BLOCK
    q_refs = (q0_ref, q1_ref, q2_ref)
    for g in range(len(patterns)):
        for rep in range(KV_REP):
            q_scr[g * KV_REP + rep] = q_refs[g][:, rep * HEAD_DIM:(rep + 1) * HEAD_DIM].astype(F32)
    k_scr[...] = k_ref[...].astype(F32)
    v_scr[...] = v_ref[...].astype(F32)

    for g, (window, dil) in enumerate(patterns):
        n_steps = window // dil
        m_len = length // dil
        nblk = m_len // blk
        win = min(2 * blk, m_len)

        def tile(idx, c, g=g, dil=dil, nblk=nblk, win=win, n_steps=n_steps):
            r = idx // nblk
            n = idx % nblk
            ks = jnp.maximum(n - 1, 0) * blk
            q0 = n * (blk * dil) + r
            k0 = ks * dil + r
            qs = [q_scr[g * KV_REP + rep, pl.ds(q0, blk, stride=dil), :] for rep in range(KV_REP)]
            q2 = jnp.concatenate(qs, axis=0).astype(BF16)
            kk = k_scr[pl.ds(k0, win, stride=dil), :].astype(BF16)
            vv = v_scr[pl.ds(k0, win, stride=dil), :].astype(BF16)
            s = lax.dot_general(q2, kk, (((1,), (1,)), ((), ())), preferred_element_type=F32)
            qi = lax.broadcasted_iota(I32, s.shape, 0) & (blk - 1)
            kj = lax.broadcasted_iota(I32, s.shape, 1)
            dist = (n * blk + qi) - (ks + kj)
            s = jnp.where((dist >= 0) & (dist <= n_steps), s, NEG_INF)
            m = jnp.max(s, axis=-1, keepdims=True)
            e = jnp.exp(s - m)
            den = jnp.sum(e, axis=-1, keepdims=True)
            pv = jnp.dot(e.astype(BF16), vv, preferred_element_type=F32)
            for rep in range(KV_REP):
                sl = slice(rep * blk, (rep + 1) * blk)
                rows = pl.ds(q0, blk, stride=dil)
                m_t = jnp.broadcast_to(m[sl], (blk, HEAD_DIM))
                l_t = jnp.broadcast_to(den[sl], (blk, HEAD_DIM))
                if g == 0:
                    acc_scr[rep, rows, :] = pv[sl]
                    m_scr[rep, rows, :] = m_t
                    l_scr[rep, rows, :] = l_t
                else:
                    m_old = m_scr[rep, rows, :]
                    m_new = jnp.maximum(m_old, m_t)
                    a_old = jnp.exp(m_old - m_new)
                    a_new = jnp.exp(m_t - m_new)
                    acc_scr[rep, rows, :] = acc_scr[rep, rows, :] * a_old + pv[sl] * a_new
                    l_scr[rep, rows, :] = l_scr[rep, rows, :] * a_old + l_t * a_new
                    m_scr[rep, rows, :] = m_new
            return c

        lax.fori_loop(0, dil * nblk, tile, 0)

    for rep in range(KV_REP):
        o_ref[:, rep * HEAD_DIM:(rep + 1) * HEAD_DIM] = (acc_scr[rep] / l_scr[rep]).astype(o_ref.dtype)


def dilated_attention(qkv, bsz, length):
    t = qkv.shape[0]
    qw = KV_REP * HEAD_DIM
    npat = len(DILATED_PATTERNS)
    k_base = npat * N_Q_HEADS
    v_base = k_base + N_KV_HEADS
    for _, dil in DILATED_PATTERNS:
        assert (length // dil) % ATTN_BLOCK == 0

    def q_spec(g):
        return pl.BlockSpec((length, qw), lambda b, h: (b, g * N_KV_HEADS + h))

    return pl.pallas_call(
        functools.partial(_attn_kernel, length=length, patterns=DILATED_PATTERNS),
        grid=(bsz, N_KV_HEADS),
        in_specs=[
            q_spec(0), q_spec(1), q_spec(2),
            pl.BlockSpec((length, HEAD_DIM), lambda b, h: (b, k_base + h)),
            pl.BlockSpec((length, HEAD_DIM), lambda b, h: (b, v_base + h)),
        ],
        out_specs=pl.BlockSpec((length, qw), lambda b, h: (b, h)),
        out_shape=jax.ShapeDtypeStruct((t, N_Q_HEADS * HEAD_DIM), BF16),
        scratch_shapes=[
            pltpu.VMEM((npat * KV_REP, length, HEAD_DIM), F32),
            pltpu.VMEM((length, HEAD_DIM), F32),
            pltpu.VMEM((length, HEAD_DIM), F32),
            pltpu.VMEM((KV_REP, length, HEAD_DIM), F32),
            pltpu.VMEM((KV_REP, length, HEAD_DIM), F32),
            pltpu.VMEM((KV_REP, length, HEAD_DIM), F32),
        ],
        compiler_params=_params(("parallel", "parallel")),
        name="dilated_attention",
    )(qkv, qkv, qkv, qkv, qkv)


def _proj_residual_kernel(a_ref, w_ref, x_ref, o_ref):
    o_ref[...] = x_ref[...] + jnp.dot(a_ref[...], w_ref[...], preferred_element_type=F32)


def proj_residual(a, w, x, tm=1024, tn=512):
    t, d = x.shape
    kdim = a.shape[1]
    return pl.pallas_call(
        _proj_residual_kernel,
        grid=(t // tm, d // tn),
        in_specs=[
            pl.BlockSpec((tm, kdim), lambda i, j: (i, 0)),
            pl.BlockSpec((kdim, tn), lambda i, j: (0, j)),
            pl.BlockSpec((tm, tn), lambda i, j: (i, j)),
        ],
        out_specs=pl.BlockSpec((tm, tn), lambda i, j: (i, j)),
        out_shape=jax.ShapeDtypeStruct((t, d), F32),
        compiler_params=_params(("parallel", "arbitrary")),
        name="proj_residual",
    )(a, w, x)


def _rope_tables(length):
    inv = 1.0 / (ROPE_THETA ** (jnp.arange(0, HEAD_DIM, 2, dtype=F32) / HEAD_DIM))
    ang = jnp.arange(length, dtype=F32)[:, None] * inv[None, :]
    cos, sin = jnp.cos(ang), jnp.sin(ang)
    return jnp.concatenate([cos, cos], axis=-1), jnp.concatenate([-sin, sin], axis=-1)


def kernel(x, p, s5_norm_g, s5_a_re, s5_a_im, s5_log_dt, s5_b_re, s5_b_im, s5_c_re, s5_c_im, s5_d, s5_w_glu, s5_b_glu, kv_norm_g, w_k, w_v, attn_norm_g, w_q, w_o, ffn_norm_g, router_grp_w, router_grp_b, router_exp_w, router_exp_b, w_gate, w_up, w_down, ple_norm_g, ple_gate_w, ple_proj_w, final_norm_g):
    bsz, length, d = x.shape
    depth = p.shape[0]
    n_s5 = s5_norm_g.shape[0]
    t = bsz * length
    h = x.reshape(t, d)
    cos_f, sin_f = _rope_tables(length)
    for i in range(depth):
        if i < n_s5:
            lam_re, lam_im, w_in, w_out = s5_prepare(
                s5_a_re[i], s5_a_im[i], s5_log_dt[i], s5_b_re[i], s5_b_im[i], s5_c_re[i], s5_c_im[i])
            u = rmsnorm(h, s5_norm_g[i])
            g = s5_core(u.reshape(bsz, length, d), lam_re, lam_im, w_in, w_out, s5_d[i])
            h = glu_residual(g.reshape(t, d), s5_w_glu[i].astype(BF16), s5_b_glu[i], h)
        else:
            j = i - n_s5
            assert j == 0, "shared K/V reuse across several attention layers is not implemented"
            w_all = jnp.concatenate([w_q[j], w_k, w_v], axis=1).astype(BF16)
            qkv = qkv_project(h, attn_norm_g[j], kv_norm_g, w_all, cos_f, sin_f, length)
            attn = dilated_attention(qkv, bsz, length)
            h = proj_residual(attn, w_o[j].astype(BF16), h)
        h = moe_ple_layer(
            h, p[i].reshape(t, -1), ffn_norm_g[i], router_grp_w[i], router_grp_b[i], router_exp_w[i],
            router_exp_b[i], w_gate[i], w_up[i], w_down[i], ple_norm_g[i], ple_gate_w[i].astype(BF16),
            ple_proj_w[i].astype(BF16), final_norm_g, final_norm=(i == depth - 1))
    return h.reshape(bsz, length, d)
```

```python
import functools
import math

import jax
import jax.numpy as jnp
from jax import lax
from jax.experimental import pallas as pl
from jax.experimental.pallas import tpu as pltpu

F32 = jnp.float32
BF16 = jnp.bfloat16
I32 = jnp.int32

NORM_EPS = 1e-6
LANES = 128
SUBLANES = 8
MXU_DIM = 256
VMEM_LIMIT_BYTES = 56 * 1024 * 1024

SSM_GROUP = 16
SSM_STATE = 64
SSM_GROUP_BLOCK = MXU_DIM // SSM_GROUP
SSM_CHUNK = 128
HEAD_DIM = 128
N_Q_HEADS = 16
N_KV_HEADS = 8
KV_REP = N_Q_HEADS // N_KV_HEADS
DILATED_PATTERNS = ((128, 1), (512, 4), (2048, 16))
ATTN_BLOCK = 128
ROPE_THETA = 10000.0
NEG_INF = -1e30
N_GROUPS = 4
EXPERTS_PER_GROUP = 8
N_EXPERTS = N_GROUPS * EXPERTS_PER_GROUP
MOE_TILE = 256
ROUTER_TILE = 256


def _params(sem):
    return pltpu.CompilerParams(dimension_semantics=sem, vmem_limit_bytes=VMEM_LIMIT_BYTES)


def _rms_scale(x):
    return lax.rsqrt(jnp.mean(x * x, axis=-1, keepdims=True) + NORM_EPS)


def _rmsnorm_kernel(x_ref, g_ref, o_ref):
    x = x_ref[...]
    o_ref[...] = (x * _rms_scale(x) * g_ref[...]).astype(o_ref.dtype)


def rmsnorm(x, g, out_dtype=F32, tm=512):
    t, d = x.shape
    return pl.pallas_call(
        _rmsnorm_kernel,
        grid=(t // tm,),
        in_specs=[pl.BlockSpec((tm, d), lambda i: (i, 0)),
                  pl.BlockSpec((1, d), lambda i: (0, 0))],
        out_specs=pl.BlockSpec((tm, d), lambda i: (i, 0)),
        out_shape=jax.ShapeDtypeStruct((t, d), out_dtype),
        compiler_params=_params(("parallel",)),
        name="rmsnorm",
    )(x, g.reshape(1, d))


def _s5_discretize_kernel(are_ref, aim_ref, ldt_ref, bre_ref, bim_ref,
                          lre_ref, lim_ref, bbre_ref, bbim_ref):
    a_re, a_im = are_ref[...], aim_ref[...]
    dt = jnp.exp(ldt_ref[...])
    mag = jnp.exp(dt * a_re)
    l_re = mag * jnp.cos(dt * a_im)
    l_im = mag * jnp.sin(dt * a_im)
    n_re, n_im = l_re - 1.0, l_im
    den = a_re * a_re + a_im * a_im
    f_re = (n_re * a_re + n_im * a_im) / den
    f_im = (n_im * a_re - n_re * a_im) / den
    b_re, b_im = bre_ref[...], bim_ref[...]
    lre_ref[...] = l_re
    lim_ref[...] = l_im
    bbre_ref[...] = f_re * b_re - f_im * b_im
    bbim_ref[...] = f_re * b_im + f_im * b_re


def _block_diag(m):
    nb, gb, a, b = m.shape
    eye = jnp.eye(gb, dtype=m.dtype)
    return jnp.einsum("ngab,gh->ngahb", m, eye).reshape(nb, gb * a, gb * b)


def s5_prepare(a_re, a_im, log_dt, b_re, b_im, c_re, c_im):
    g, p = a_re.shape
    gc = b_re.shape[-1]
    rep = lambda v: jnp.repeat(v, gc, axis=-1)
    shp = jax.ShapeDtypeStruct((g, p * gc), F32)
    l_re, l_im, bb_re, bb_im = pl.pallas_call(
        _s5_discretize_kernel, out_shape=(shp, shp, shp, shp), name="s5_discretize",
    )(rep(a_re), rep(a_im), rep(jnp.broadcast_to(log_dt[:, None], (g, p))),
      b_re.reshape(g, p * gc), b_im.reshape(g, p * gc))
    nb = g // SSM_GROUP_BLOCK
    lam_re = l_re[:, ::gc].reshape(nb, 1, SSM_GROUP_BLOCK * p)
    lam_im = l_im[:, ::gc].reshape(nb, 1, SSM_GROUP_BLOCK * p)

    def in_mat(bb):
        m = bb.reshape(g, p, gc).transpose(0, 2, 1).reshape(nb, SSM_GROUP_BLOCK, gc, p)
        return _block_diag(m)

    def out_mat(c):
        m = c.transpose(0, 2, 1).reshape(nb, SSM_GROUP_BLOCK, p, gc)
        return _block_diag(m)

    w_in = jnp.concatenate([in_mat(bb_re), in_mat(bb_im)], axis=-1).astype(BF16)
    w_out = jnp.concatenate([out_mat(c_re), out_mat(-c_im)], axis=1).astype(BF16)
    return lam_re, lam_im, w_in, w_out


def _s5_kernel(u_ref, win_ref, lre_ref, lim_ref, wout_ref, d_ref, o_ref,
               lhs_scr, st_scr, bu_scr, y_scr, *, nb, ch, half):
    k = pl.program_id(1)
    rows = nb * ch
    nslab = lhs_scr.shape[0]

    @pl.when(k == 0)
    def _():
        st_scr[...] = jnp.zeros_like(st_scr)

    for b in range(nb):
        for s in range(nslab):
            lhs_scr[s, pl.ds(b, ch, stride=nb), :] = u_ref[b, :, s * LANES:(s + 1) * LANES]
    u_tm = jnp.concatenate([lhs_scr[s] for s in range(nslab)], axis=1)
    bu_scr[...] = jnp.dot(u_tm.astype(BF16), win_ref[0], preferred_element_type=F32)

    l_re = jnp.broadcast_to(lre_ref[0], (nb, half))
    l_im = jnp.broadcast_to(lim_ref[0], (nb, half))

    def step(t, carry):
        s_re, s_im = carry
        r0 = pl.multiple_of(t * nb, nb)
        n_re = l_re * s_re - l_im * s_im + bu_scr[pl.ds(r0, nb), 0:half]
        n_im = l_re * s_im + l_im * s_re + bu_scr[pl.ds(r0, nb), half:2 * half]
        bu_scr[pl.ds(r0, nb), 0:half] = n_re
        bu_scr[pl.ds(r0, nb), half:2 * half] = n_im
        return n_re, n_im

    s_re, s_im = lax.fori_loop(0, ch, step, (st_scr[:, 0:half], st_scr[:, half:2 * half]), unroll=2)
    st_scr[:, 0:half] = s_re
    st_scr[:, half:2 * half] = s_im

    y = jnp.dot(bu_scr[...].astype(BF16), wout_ref[0], preferred_element_type=F32)
    y = jax.nn.gelu(y + d_ref[...] * u_tm)
    for s in range(nslab):
        y_scr[s] = y[:, s * LANES:(s + 1) * LANES]
    for b in range(nb):
        for s in range(nslab):
            o_ref[b, :, s * LANES:(s + 1) * LANES] = (
                y_scr[s, pl.ds(b, ch, stride=nb), :].astype(o_ref.dtype))


def s5_core(u3, lam_re, lam_im, w_in, w_out, d_skip):
    nb, length, d = u3.shape
    assert nb == SUBLANES, "the scan keeps one batch row per sublane"
    cb = w_in.shape[1]
    half = w_in.shape[2] // 2
    ch = SSM_CHUNK
    kern = functools.partial(_s5_kernel, nb=nb, ch=ch, half=half)
    return pl.pallas_call(
        kern,
        grid=(d // cb, length // ch),
        in_specs=[
            pl.BlockSpec((nb, ch, cb), lambda g, k: (0, k, g)),
            pl.BlockSpec((1, cb, 2 * half), lambda g, k: (g, 0, 0)),
            pl.BlockSpec((1, 1, half), lambda g, k: (g, 0, 0)),
            pl.BlockSpec((1, 1, half), lambda g, k: (g, 0, 0)),
            pl.BlockSpec((1, 2 * half, cb), lambda g, k: (g, 0, 0)),
            pl.BlockSpec((1, cb), lambda g, k: (0, g)),
        ],
        out_specs=pl.BlockSpec((nb, ch, cb), lambda g, k: (0, k, g)),
        out_shape=jax.ShapeDtypeStruct((nb, length, d), BF16),
        scratch_shapes=[
            pltpu.VMEM((cb // LANES, nb * ch, LANES), F32),
            pltpu.VMEM((nb, 2 * half), F32),
            pltpu.VMEM((nb * ch, 2 * half), F32),
            pltpu.VMEM((cb // LANES, nb * ch, LANES), F32),
        ],
        compiler_params=_params(("parallel", "arbitrary")),
        name="s5_core",
    )(u3, w_in, lam_re, lam_im, w_out, d_skip.reshape(1, d))


def _glu_kernel(g_ref, wv_ref, wg_ref, bv_ref, bg_ref, x_ref, o_ref):
    a = g_ref[...]
    val = jnp.dot(a, wv_ref[...], preferred_element_type=F32) + bv_ref[...]
    gate = jnp.dot(a, wg_ref[...], preferred_element_type=F32) + bg_ref[...]
    o_ref[...] = x_ref[...] + val * jax.nn.sigmoid(gate)


def glu_residual(g, w_glu, b_glu, x, tm=1024, tn=512):
    t, d = x.shape
    nj = d // tn
    b2 = b_glu.reshape(1, 2 * d)
    return pl.pallas_call(
        _glu_kernel,
        grid=(t // tm, nj),
        in_specs=[
            pl.BlockSpec((tm, d), lambda i, j: (i, 0)),
            pl.BlockSpec((d, tn), lambda i, j: (0, j)),
            pl.BlockSpec((d, tn), lambda i, j: (0, j + nj)),
            pl.BlockSpec((1, tn), lambda i, j: (0, j)),
            pl.BlockSpec((1, tn), lambda i, j: (0, j + nj)),
            pl.BlockSpec((tm, tn), lambda i, j: (i, j)),
        ],
        out_specs=pl.BlockSpec((tm, tn), lambda i, j: (i, j)),
        out_shape=jax.ShapeDtypeStruct((t, d), F32),
        compiler_params=_params(("parallel", "arbitrary")),
        name="glu_residual",
    )(g, w_glu, w_glu, b2, b2, x)


def _store_token_major(ref, row0, x):
    n, d = x.shape
    nchunk = d // LANES
    for c in range(nchunk):
        ref[pl.ds(row0 + c, n, stride=nchunk), :] = x[:, c * LANES:(c + 1) * LANES]


def _load_token_major(ref, row0, n, d):
    nchunk = d // LANES
    return jnp.concatenate([ref[pl.ds(row0 + c, n, stride=nchunk), :] for c in range(nchunk)], axis=1)


def _router_kernel(h_ref, g_ref, wr_ref, br_ref, u_ref, meta_ref, cnt_ref, base_scr, *, tm):
    i = pl.program_id(0)

    @pl.when(i == 0)
    def _():
        base_scr[...] = jnp.zeros_like(base_scr)

    x = h_ref[...]
    u = x * _rms_scale(x) * g_ref[...]
    _store_token_major(u_ref, 0, u)
    logits = jnp.dot(u, wr_ref[...], preferred_element_type=F32,
                     precision=lax.Precision.HIGHEST) + br_ref[...]
    lane = lax.broadcasted_iota(I32, logits.shape, 1)
    big = jnp.int32(1 << 20)
    is_grp = (lane >= N_EXPERTS) & (lane < N_EXPERTS + N_GROUPS)
    gl = jnp.where(is_grp, logits, -jnp.inf)
    gmax = jnp.max(gl, axis=-1, keepdims=True)
    gidx = jnp.min(jnp.where(gl == gmax, lane, big), axis=-1, keepdims=True) - N_EXPERTS
    gsum = jnp.sum(jnp.where(is_grp, jnp.exp(logits - gmax), 0.0), axis=-1, keepdims=True)
    grp_w = 1.0 / gsum
    in_grp = (lane < N_EXPERTS) & (jnp.right_shift(lane, EXPERTS_PER_GROUP.bit_length() - 1) == gidx)
    sel = jnp.where(in_grp, logits, -jnp.inf)
    v1 = jnp.max(sel, axis=-1, keepdims=True)
    i1 = jnp.min(jnp.where(sel == v1, lane, big), axis=-1, keepdims=True)
    sel2 = jnp.where(lane == i1, -jnp.inf, sel)
    v2 = jnp.max(sel2, axis=-1, keepdims=True)
    i2 = jnp.min(jnp.where(sel2 == v2, lane, big), axis=-1, keepdims=True)
    e21 = jnp.exp(v2 - v1)
    w1 = grp_w / (1.0 + e21)
    w2 = grp_w * e21 / (1.0 + e21)
    oh1 = lane == i1
    oh2 = lane == i2
    oh = (oh1 | oh2).astype(F32)
    rr = lax.broadcasted_iota(I32, (tm, tm), 0)
    cc = lax.broadcasted_iota(I32, (tm, tm), 1)
    tri = (cc < rr).astype(BF16)
    before = jnp.dot(tri, oh.astype(BF16), preferred_element_type=F32) + base_scr[...]
    r1 = jnp.sum(jnp.where(oh1, before, 0.0), axis=-1, keepdims=True)
    r2 = jnp.sum(jnp.where(oh2, before, 0.0), axis=-1, keepdims=True)
    base_scr[...] = base_scr[...] + jnp.sum(oh, axis=0, keepdims=True)
    cnt_ref[...] = base_scr[...]
    meta = jnp.where(lane == 0, i1.astype(F32), 0.0)
    meta = jnp.where(lane == 1, i2.astype(F32), meta)
    meta = jnp.where(lane == 2, w1, meta)
    meta = jnp.where(lane == 3, w2, meta)
    meta = jnp.where(lane == 4, r1, meta)
    meta = jnp.where(lane == 5, r2, meta)
    meta_ref[...] = meta


def moe_router(h, norm_g, r_grp_w, r_grp_b, r_exp_w, r_exp_b):
    t, d = h.shape
    tm = ROUTER_TILE
    pad = LANES - N_EXPERTS - N_GROUPS
    wr = jnp.concatenate([r_exp_w, r_grp_w, jnp.zeros((d, pad), F32)], axis=1)
    br = jnp.concatenate([r_exp_b, r_grp_b, jnp.zeros((pad,), F32)]).reshape(1, LANES)
    return pl.pallas_call(
        functools.partial(_router_kernel, tm=tm),
        grid=(t // tm,),
        in_specs=[
            pl.BlockSpec((tm, d), lambda i: (i, 0)),
            pl.BlockSpec((1, d), lambda i: (0, 0)),
            pl.BlockSpec((d, LANES), lambda i: (0, 0)),
            pl.BlockSpec((1, LANES), lambda i: (0, 0)),
        ],
        out_specs=[
            pl.BlockSpec((tm * (d // LANES), LANES), lambda i: (i, 0)),
            pl.BlockSpec((tm, LANES), lambda i: (i, 0)),
            pl.BlockSpec((1, LANES), lambda i: (0, 0)),
        ],
        out_shape=[
            jax.ShapeDtypeStruct((t * (d // LANES), LANES), F32),
            jax.ShapeDtypeStruct((t, LANES), F32),
            jax.ShapeDtypeStruct((1, LANES), F32),
        ],
        scratch_shapes=[pltpu.VMEM((1, LANES), F32)],
        compiler_params=_params(("arbitrary",)),
        name="moe_router",
    )(h, norm_g.reshape(1, d), wr, br)


def moe_plan(meta, counts, t):
    tile = MOE_TILE
    n_tiles = (2 * t) // tile + N_EXPERTS
    cnt = counts[0, :N_EXPERTS].astype(I32)
    padded = ((cnt + tile - 1) // tile) * tile
    ends = jnp.cumsum(padded)
    offs = ends - padded
    e = meta[:, 0:2].astype(I32)
    rank = meta[:, 4:6].astype(I32)
    slot = offs[e] + rank
    row = jnp.arange(t, dtype=I32)[:, None] + jnp.array([0, t], I32)[None, :]
    row_of_slot = jnp.full((n_tiles * tile,), -1, I32).at[slot.reshape(-1)].set(row.reshape(-1))
    row_of_slot = row_of_slot.reshape(n_tiles, tile)
    valid = row_of_slot >= 0
    src = jnp.where(valid, jnp.where(row_of_slot >= t, row_of_slot - t, row_of_slot), 0)
    spare = 2 * t + (jnp.arange(n_tiles, dtype=I32)[:, None] % 2) * tile + jnp.arange(tile, dtype=I32)[None, :]
    dst = jnp.where(valid, row_of_slot, spare)
    slot_io = jnp.concatenate([src, dst], axis=1).reshape(n_tiles, 1, 2 * tile)
    starts = jnp.arange(n_tiles, dtype=I32) * tile
    tile_expert = jnp.minimum(
        jnp.sum((starts[:, None] >= ends[None, :]).astype(I32), axis=1), N_EXPERTS - 1)
    n_used = (ends[-1] // tile).astype(I32).reshape(1)
    return slot_io, tile_expert, n_used


def _moe_expert_kernel(te_ref, nu_ref, cur_ref, nxt_ref, u_hbm, wg_ref, wu_ref, wd_ref, o_hbm,
                       idx0, idx1, rows, yrows, wg_s, wu_s, wd_s, sem_i, sem_g, sem_s, *, tile, d, t):
    i = pl.program_id(0)
    n_used = nu_ref[0]
    nchunk = d // LANES
    buf_rows = tile * nchunk
    cur = i % 2
    nxt = 1 - cur
    idx_smem = (idx0, idx1)

    def by_parity(par, fn):
        for b in (0, 1):
            pl.when(par == b)(functools.partial(fn, b))

    def idx_copy(src_ref, b):
        return pltpu.make_async_copy(src_ref.at[0, 0], idx_smem[b], sem_i.at[b])

    def gather_copy(tok, r, b):
        src = u_hbm.at[pl.ds(pl.multiple_of(tok * nchunk, nchunk), nchunk)]
        dst = rows.at[pl.ds(pl.multiple_of(b * buf_rows + r * nchunk, nchunk), nchunk)]
        return pltpu.make_async_copy(src, dst, sem_g.at[b])

    def scatter_copy(row, r, b):
        src = yrows.at[pl.ds(pl.multiple_of(b * buf_rows + r * nchunk, nchunk), nchunk)]
        dst = o_hbm.at[pl.ds(pl.multiple_of(row * nchunk, nchunk), nchunk)]
        return pltpu.make_async_copy(src, dst, sem_s.at[b])

    def for_rows(fn):
        def body(r, c):
            fn(r)
            return c
        lax.fori_loop(0, tile, body, 0, unroll=8)

    def start_gather(b):
        for_rows(lambda r: gather_copy(idx_smem[b][r], r, b).start())

    def wait_gather(b):
        for_rows(lambda r: gather_copy(0, r, b).wait())

    def start_scatter(b):
        for_rows(lambda r: scatter_copy(idx_smem[b][tile + r], r, b).start())

    def wait_scatter(b):
        for_rows(lambda r: scatter_copy(0, r, b).wait())

    @pl.when(i < n_used)
    def _():
        @pl.when(i == 0)
        def _():
            first = idx_copy(cur_ref, 0)
            first.start()
            first.wait()
            start_gather(0)
            yrows[pl.ds(0, buf_rows), :] = jnp.zeros((buf_rows, LANES), F32)
            fills = [pltpu.make_async_copy(yrows.at[pl.ds(0, buf_rows)],
                                           o_hbm.at[pl.ds((2 * t + b * tile) * nchunk, buf_rows)], sem_s.at[b])
                     for b in (0, 1)]
            for cp in fills:
                cp.start()
            for cp in fills:
                cp.wait()

        has_next = i + 1 < n_used

        @pl.when(has_next)
        def _():
            by_parity(nxt, lambda b: idx_copy(nxt_ref, b).start())

        by_parity(cur, wait_gather)

        prev = te_ref[jnp.maximum(i - 1, 0)]

        @pl.when((i == 0) | (te_ref[i] != prev))
        def _():
            wg_s[...] = wg_ref[0, 0].astype(BF16)
            wu_s[...] = wu_ref[0, 0].astype(BF16)
            wd_s[...] = wd_ref[0, 0].astype(BF16)

        @pl.when(has_next)
        def _():
            def prefetch(b):
                idx_copy(nxt_ref, b).wait()
                start_gather(b)
            by_parity(nxt, prefetch)

        x = _load_token_major(rows, pl.multiple_of(cur * buf_rows, buf_rows), tile, d).astype(BF16)
        gate = jnp.dot(x, wg_s[...], preferred_element_type=F32)
        up = jnp.dot(x, wu_s[...], preferred_element_type=F32)
        hid = (jax.nn.silu(gate) * up).astype(BF16)
        y = jnp.dot(hid, wd_s[...], preferred_element_type=F32)

        @pl.when(i >= 2)
        def _():
            by_parity(cur, wait_scatter)

        _store_token_major(yrows, pl.multiple_of(cur * buf_rows, buf_rows), y)
        by_parity(cur, start_scatter)

        @pl.when(i == n_used - 1)
        def _():
            @pl.when(i >= 1)
            def _():
                by_parity(nxt, wait_scatter)

            by_parity(cur, wait_scatter)


def moe_experts(u_rows, slot_io, tile_expert, n_used, w_gate, w_up, w_down, layer, t, d):
    n_tiles, _, two_tile = slot_io.shape
    tile = two_tile // 2
    hdim = w_gate.shape[-1]
    nchunk = d // LANES
    last = n_tiles - 1
    grid_spec = pltpu.PrefetchScalarGridSpec(
        num_scalar_prefetch=2,
        grid=(n_tiles,),
        in_specs=[
            pl.BlockSpec((1, 1, two_tile), lambda i, te, nu: (i, 0, 0)),
            pl.BlockSpec((1, 1, two_tile), lambda i, te, nu: (jnp.minimum(i + 1, last), 0, 0)),
            pl.BlockSpec(memory_space=pl.ANY),
            pl.BlockSpec((1, 1, d, hdim), lambda i, te, nu: (layer, te[i], 0, 0)),
            pl.BlockSpec((1, 1, d, hdim), lambda i, te, nu: (layer, te[i], 0, 0)),
            pl.BlockSpec((1, 1, hdim, d), lambda i, te, nu: (layer, te[i], 0, 0)),
        ],
        out_specs=pl.BlockSpec(memory_space=pl.ANY),
        scratch_shapes=[
            pltpu.SMEM((two_tile,), I32),
            pltpu.SMEM((two_tile,), I32),
            pltpu.VMEM((2 * tile * nchunk, LANES), F32),
            pltpu.VMEM((2 * tile * nchunk, LANES), F32),
            pltpu.VMEM((d, hdim), BF16),
            pltpu.VMEM((d, hdim), BF16),
            pltpu.VMEM((hdim, d), BF16),
            pltpu.SemaphoreType.DMA((2,)),
            pltpu.SemaphoreType.DMA((2,)),
            pltpu.SemaphoreType.DMA((2,)),
        ],
    )
    return pl.pallas_call(
        functools.partial(_moe_expert_kernel, tile=tile, d=d, t=t),
        grid_spec=grid_spec,
        out_shape=jax.ShapeDtypeStruct(((2 * t + 2 * tile) * nchunk, LANES), F32),
        compiler_params=_params(("arbitrary",)),
        name="moe_experts",
    )(tile_expert, n_used, slot_io, slot_io, u_rows, w_gate, w_up, w_down)


def _combine_ple_kernel(y1_ref, y2_ref, h_ref, meta_ref, p_ref, g_ref, wgate_ref, wproj_ref, fg_ref,
                        o_ref, *, tm, final_norm):
    d = h_ref.shape[1]
    meta = meta_ref[...]
    y1 = _load_token_major(y1_ref, 0, tm, d)
    y2 = _load_token_major(y2_ref, 0, tm, d)
    h2 = h_ref[...] + (meta[:, 2:3] * y1 + meta[:, 3:4] * y2)
    a = (h2 * _rms_scale(h2) * g_ref[...]).astype(BF16)
    gate = jax.nn.sigmoid(jnp.dot(a, wgate_ref[...], preferred_element_type=F32))
    proj = jnp.dot(p_ref[...].astype(BF16), wproj_ref[...], preferred_element_type=F32)
    h3 = h2 + gate * proj
    if final_norm:
        h3 = h3 * _rms_scale(h3) * fg_ref[...]
    o_ref[...] = h3


def moe_combine_ple(y_rows, h, meta, p_all, layer, ple_norm_g, ple_gate_w, ple_proj_w, final_g, final_norm,
                    tm=256):
    t, d = h.shape
    pd = p_all.shape[-1]
    nchunk = d // LANES
    nt = t // tm
    return pl.pallas_call(
        functools.partial(_combine_ple_kernel, tm=tm, final_norm=final_norm),
        grid=(nt,),
        in_specs=[
            pl.BlockSpec((tm * nchunk, LANES), lambda i: (i, 0)),
            pl.BlockSpec((tm * nchunk, LANES), lambda i: (nt + i, 0)),
            pl.BlockSpec((tm, d), lambda i: (i, 0)),
            pl.BlockSpec((tm, LANES), lambda i: (i, 0)),
            pl.BlockSpec((tm, pd), lambda i: (layer * nt + i, 0)),
            pl.BlockSpec((1, d), lambda i: (0, 0)),
            pl.BlockSpec((d, d), lambda i: (0, 0)),
            pl.BlockSpec((pd, d), lambda i: (0, 0)),
            pl.BlockSpec((1, d), lambda i: (0, 0)),
        ],
        out_specs=pl.BlockSpec((tm, d), lambda i: (i, 0)),
        out_shape=jax.ShapeDtypeStruct((t, d), F32),
        compiler_params=_params(("parallel",)),
        name="moe_combine_ple",
    )(y_rows, y_rows, h, meta, p_all, ple_norm_g.reshape(1, d), ple_gate_w, ple_proj_w, final_g.reshape(1, d))


def moe_ple_layer(h, p_all, layer, ffn_norm_g, r_grp_w, r_grp_b, r_exp_w, r_exp_b, w_gate, w_up, w_down,
                  ple_norm_g, ple_gate_w, ple_proj_w, final_g, final_norm):
    t, d = h.shape
    u_rows, meta, counts = moe_router(h, ffn_norm_g, r_grp_w, r_grp_b, r_exp_w, r_exp_b)
    slot_io, tile_expert, n_used = moe_plan(meta, counts, t)
    y_rows = moe_experts(u_rows, slot_io, tile_expert, n_used, w_gate, w_up, w_down, layer, t, d)
    return moe_combine_ple(y_rows, h, meta, p_all, layer, ple_norm_g, ple_gate_w, ple_proj_w, final_g,
                           final_norm)


def _qkv_kernel(x_ref, gq_ref, gkv_ref, w_ref, cos_ref, sin_ref, o_ref, act_scr, *, nq_tiles, scale):
    j = pl.program_id(1)

    @pl.when(j == 0)
    def _():
        x = x_ref[...]
        xn = x * _rms_scale(x)
        act_scr[0] = (xn * gq_ref[...]).astype(BF16)
        act_scr[1] = (xn * gkv_ref[...]).astype(BF16)

    a = act_scr[(j >= nq_tiles).astype(I32)]
    acc = jnp.dot(a, w_ref[...], preferred_element_type=F32)
    tn = acc.shape[1]
    reps = tn // HEAD_DIM
    cos = jnp.tile(cos_ref[...], (1, reps))
    sin = jnp.tile(sin_ref[...], (1, reps))
    lane = lax.broadcasted_iota(I32, acc.shape, 1)
    first_half = (lane & (HEAD_DIM - 1)) < (HEAD_DIM // 2)
    partner = jnp.where(first_half, pltpu.roll(acc, tn - HEAD_DIM // 2, 1), pltpu.roll(acc, HEAD_DIM // 2, 1))
    roped = acc * cos + partner * sin
    out = jnp.where(j <= nq_tiles, roped, acc)
    out = out * jnp.where(j < nq_tiles, scale, 1.0)
    o_ref[...] = out.astype(o_ref.dtype)


def qkv_project(h, attn_g, kv_g, w_all, cos_f, sin_f, length, tm=512, tn=1024):
    t, d = h.shape
    n_all = w_all.shape[1]
    kvw = N_KV_HEADS * HEAD_DIM
    assert tn == kvw
    nq_tiles = (n_all - 2 * kvw) // tn
    nl = length // tm
    return pl.pallas_call(
        functools.partial(_qkv_kernel, nq_tiles=nq_tiles, scale=HEAD_DIM ** -0.5),
        grid=(t // tm, n_all // tn),
        in_specs=[
            pl.BlockSpec((tm, d), lambda i, j: (i, 0)),
            pl.BlockSpec((1, d), lambda i, j: (0, 0)),
            pl.BlockSpec((1, d), lambda i, j: (0, 0)),
            pl.BlockSpec((d, tn), lambda i, j: (0, j)),
            pl.BlockSpec((tm, HEAD_DIM), lambda i, j: (i % nl, 0)),
            pl.BlockSpec((tm, HEAD_DIM), lambda i, j: (i % nl, 0)),
        ],
        out_specs=pl.BlockSpec((tm, tn), lambda i, j: (i, j)),
        out_shape=jax.ShapeDtypeStruct((t, n_all), BF16),
        scratch_shapes=[pltpu.VMEM((2, tm, d), BF16)],
        compiler_params=_params(("parallel", "arbitrary")),
        name="qkv_project",
    )(h, attn_g.reshape(1, d), kv_g.reshape(1, d), w_all, cos_f, sin_f)


def _attn_kernel(q0_ref, q1_ref, q2_ref, k_ref, v_ref, o_ref, q_scr, k_scr, v_scr, acc_scr, m_scr, l_scr,
                 *, length, patterns):
    blk = ATTN_BLOCK
    q_refs = (q0_ref, q1_ref, q2_ref)
    for g in range(len(patterns)):
        for rep in range(KV_REP):
            q_scr[g * KV_REP + rep] = q_refs[g][:, rep * HEAD_DIM:(rep + 1) * HEAD_DIM].astype(F32)
    k_scr[...] = k_ref[...].astype(F32)
    v_scr[...] = v_ref[...].astype(F32)

    for g, (window, dil) in enumerate(patterns):
        n_steps = window // dil
        m_len = length // dil
        nblk = m_len // blk
        win = min(2 * blk, m_len)

        def tile(idx, c, g=g, dil=dil, nblk=nblk, win=win, n_steps=n_steps):
            r = idx // nblk
            n = idx % nblk
            ks = jnp.maximum(n - 1, 0) * blk
            q0 = n * (blk * dil) + r
            k0 = ks * dil + r
            qs = [q_scr[g * KV_REP + rep, pl.ds(q0, blk, stride=dil), :] for rep in range(KV_REP)]
            q2 = jnp.concatenate(qs, axis=0).astype(BF16)
            kk = k_scr[pl.ds(k0, win, stride=dil), :].astype(BF16)
            vv = v_scr[pl.ds(k0, win, stride=dil), :].astype(BF16)
            s = lax.dot_general(q2, kk, (((1,), (1,)), ((), ())), preferred_element_type=F32)
            qi = lax.broadcasted_iota(I32, s.shape, 0) & (blk - 1)
            kj = lax.broadcasted_iota(I32, s.shape, 1)
            dist = (n * blk + qi) - (ks + kj)
            s = jnp.where((dist >= 0) & (dist <= n_steps), s, NEG_INF)
            m = jnp.max(s, axis=-1, keepdims=True)
            e = jnp.exp(s - m)
            den = jnp.sum(e, axis=-1, keepdims=True)
            pv = jnp.dot(e.astype(BF16), vv, preferred_element_type=F32)
            for rep in range(KV_REP):
                sl = slice(rep * blk, (rep + 1) * blk)
                rows = pl.ds(q0, blk, stride=dil)
                m_t = jnp.broadcast_to(m[sl], (blk, HEAD_DIM))
                l_t = jnp.broadcast_to(den[sl], (blk, HEAD_DIM))
                if g == 0:
                    acc_scr[rep, rows, :] = pv[sl]
                    m_scr[rep, rows, :] = m_t
                    l_scr[rep, rows, :] = l_t
                else:
                    m_old = m_scr[rep, rows, :]
                    m_new = jnp.maximum(m_old, m_t)
                    a_old = jnp.exp(m_old - m_new)
                    a_new = jnp.exp(m_t - m_new)
                    acc_scr[rep, rows, :] = acc_scr[rep, rows, :] * a_old + pv[sl] * a_new
                    l_scr[rep, rows, :] = l_scr[rep, rows, :] * a_old + l_t * a_new
                    m_scr[rep, rows, :] = m_new
            return c

        lax.fori_loop(0, dil * nblk, tile, 0, unroll=2)

    for rep in range(KV_REP):
        o_ref[:, rep * HEAD_DIM:(rep + 1) * HEAD_DIM] = (acc_scr[rep] / l_scr[rep]).astype(o_ref.dtype)


def dilated_attention(qkv, bsz, length):
    t = qkv.shape[0]
    qw = KV_REP * HEAD_DIM
    npat = len(DILATED_PATTERNS)
    k_base = npat * N_Q_HEADS
    v_base = k_base + N_KV_HEADS
    for _, dil in DILATED_PATTERNS:
        assert (length // dil) % ATTN_BLOCK == 0

    def q_spec(g):
        return pl.BlockSpec((length, qw), lambda b, h: (b, g * N_KV_HEADS + h))

    return pl.pallas_call(
        functools.partial(_attn_kernel, length=length, patterns=DILATED_PATTERNS),
        grid=(bsz, N_KV_HEADS),
        in_specs=[
            q_spec(0), q_spec(1), q_spec(2),
            pl.BlockSpec((length, HEAD_DIM), lambda b, h: (b, k_base + h)),
            pl.BlockSpec((length, HEAD_DIM), lambda b, h: (b, v_base + h)),
        ],
        out_specs=pl.BlockSpec((length, qw), lambda b, h: (b, h)),
        out_shape=jax.ShapeDtypeStruct((t, N_Q_HEADS * HEAD_DIM), BF16),
        scratch_shapes=[
            pltpu.VMEM((npat * KV_REP, length, HEAD_DIM), F32),
            pltpu.VMEM((length, HEAD_DIM), F32),
            pltpu.VMEM((length, HEAD_DIM), F32),
            pltpu.VMEM((KV_REP, length, HEAD_DIM), F32),
            pltpu.VMEM((KV_REP, length, HEAD_DIM), F32),
            pltpu.VMEM((KV_REP, length, HEAD_DIM), F32),
        ],
        compiler_params=_params(("parallel", "parallel")),
        name="dilated_attention",
    )(qkv, qkv, qkv, qkv, qkv)


def _proj_residual_kernel(a_ref, w_ref, x_ref, o_ref):
    o_ref[...] = x_ref[...] + jnp.dot(a_ref[...], w_ref[...], preferred_element_type=F32)


def proj_residual(a, w, x, tm=1024, tn=512):
    t, d = x.shape
    kdim = a.shape[1]
    return pl.pallas_call(
        _proj_residual_kernel,
        grid=(t // tm, d // tn),
        in_specs=[
            pl.BlockSpec((tm, kdim), lambda i, j: (i, 0)),
            pl.BlockSpec((kdim, tn), lambda i, j: (0, j)),
            pl.BlockSpec((tm, tn), lambda i, j: (i, j)),
        ],
        out_specs=pl.BlockSpec((tm, tn), lambda i, j: (i, j)),
        out_shape=jax.ShapeDtypeStruct((t, d), F32),
        compiler_params=_params(("parallel", "arbitrary")),
        name="proj_residual",
    )(a, w, x)


def _rope_tables(length):
    inv = 1.0 / (ROPE_THETA ** (jnp.arange(0, HEAD_DIM, 2, dtype=F32) / HEAD_DIM))
    ang = jnp.arange(length, dtype=F32)[:, None] * inv[None, :]
    cos, sin = jnp.cos(ang), jnp.sin(ang)
    return jnp.concatenate([cos, cos], axis=-1), jnp.concatenate([-sin, sin], axis=-1)


def kernel(x, p, s5_norm_g, s5_a_re, s5_a_im, s5_log_dt, s5_b_re, s5_b_im, s5_c_re, s5_c_im, s5_d, s5_w_glu, s5_b_glu, kv_norm_g, w_k, w_v, attn_norm_g, w_q, w_o, ffn_norm_g, router_grp_w, router_grp_b, router_exp_w, router_exp_b, w_gate, w_up, w_down, ple_norm_g, ple_gate_w, ple_proj_w, final_norm_g):
    bsz, length, d = x.shape
    depth = p.shape[0]
    n_s5 = s5_norm_g.shape[0]
    t = bsz * length
    h = x.reshape(t, d)
    cos_f, sin_f = _rope_tables(length)
    for i in range(depth):
        if i < n_s5:
            lam_re, lam_im, w_in, w_out = s5_prepare(
                s5_a_re[i], s5_a_im[i], s5_log_dt[i], s5_b_re[i], s5_b_im[i], s5_c_re[i], s5_c_im[i])
            u = rmsnorm(h, s5_norm_g[i])
            g = s5_core(u.reshape(bsz, length, d), lam_re, lam_im, w_in, w_out, s5_d[i])
            h = glu_residual(g.reshape(t, d), s5_w_glu[i].astype(BF16), s5_b_glu[i], h)
        else:
            j = i - n_s5
            assert j == 0, "shared K/V reuse across several attention layers is not implemented"
            w_all = jnp.concatenate([w_q[j], w_k, w_v], axis=1).astype(BF16)
            qkv = qkv_project(h, attn_norm_g[j], kv_norm_g, w_all, cos_f, sin_f, length)
            attn = dilated_attention(qkv, bsz, length)
            h = proj_residual(attn, w_o[j].astype(BF16), h)
        h = moe_ple_layer(
            h, p.reshape(depth * t, -1), i, ffn_norm_g[i], router_grp_w[i], router_grp_b[i], router_exp_w[i],
            router_exp_b[i], w_gate, w_up, w_down, ple_norm_g[i], ple_gate_w[i].astype(BF16),
            ple_proj_w[i].astype(BF16), final_norm_g, final_norm=(i == depth - 1))
    return h.reshape(bsz, length, d)
```

```python
import functools
import math

import jax
import jax.numpy as jnp
from jax import lax
from jax.experimental import pallas as pl
from jax.experimental.pallas import tpu as pltpu

F32 = jnp.float32
BF16 = jnp.bfloat16
I32 = jnp.int32
PACKED = jnp.uint32

NORM_EPS = 1e-6
LANES = 128
SUBLANES = 8
MXU_DIM = 256
VMEM_LIMIT_BYTES = 56 * 1024 * 1024

SSM_GROUP = 16
SSM_STATE = 64
SSM_GROUP_BLOCK = MXU_DIM // SSM_GROUP
SSM_CHUNK = 128
HEAD_DIM = 128
N_Q_HEADS = 16
N_KV_HEADS = 8
KV_REP = N_Q_HEADS // N_KV_HEADS
DILATED_PATTERNS = ((128, 1), (512, 4), (2048, 16))
ATTN_BLOCK = 128
MAX_CHEAP_SUBLANE_STRIDE = 4
ROPE_THETA = 10000.0
NEG_INF = -1e30
N_GROUPS = 4
EXPERTS_PER_GROUP = 8
N_EXPERTS = N_GROUPS * EXPERTS_PER_GROUP
MOE_TILE = 256
ROUTER_TILE = 256


def _params(sem):
    return pltpu.CompilerParams(dimension_semantics=sem, vmem_limit_bytes=VMEM_LIMIT_BYTES)


def _rms_scale(x):
    return lax.rsqrt(jnp.mean(x * x, axis=-1, keepdims=True) + NORM_EPS)


def _rmsnorm_kernel(x_ref, g_ref, o_ref):
    x = x_ref[...]
    o_ref[...] = (x * _rms_scale(x) * g_ref[...]).astype(o_ref.dtype)


def rmsnorm(x, g, out_dtype=F32, tm=512):
    t, d = x.shape
    return pl.pallas_call(
        _rmsnorm_kernel,
        grid=(t // tm,),
        in_specs=[pl.BlockSpec((tm, d), lambda i: (i, 0)),
                  pl.BlockSpec((1, d), lambda i: (0, 0))],
        out_specs=pl.BlockSpec((tm, d), lambda i: (i, 0)),
        out_shape=jax.ShapeDtypeStruct((t, d), out_dtype),
        compiler_params=_params(("parallel",)),
        name="rmsnorm",
    )(x, g.reshape(1, d))


def _s5_discretize_kernel(are_ref, aim_ref, ldt_ref, bre_ref, bim_ref,
                          lre_ref, lim_ref, bbre_ref, bbim_ref):
    a_re, a_im = are_ref[...], aim_ref[...]
    dt = jnp.exp(ldt_ref[...])
    mag = jnp.exp(dt * a_re)
    l_re = mag * jnp.cos(dt * a_im)
    l_im = mag * jnp.sin(dt * a_im)
    n_re, n_im = l_re - 1.0, l_im
    den = a_re * a_re + a_im * a_im
    f_re = (n_re * a_re + n_im * a_im) / den
    f_im = (n_im * a_re - n_re * a_im) / den
    b_re, b_im = bre_ref[...], bim_ref[...]
    lre_ref[...] = l_re
    lim_ref[...] = l_im
    bbre_ref[...] = f_re * b_re - f_im * b_im
    bbim_ref[...] = f_re * b_im + f_im * b_re


def _block_diag(m):
    nb, gb, a, b = m.shape
    eye = jnp.eye(gb, dtype=m.dtype)
    return jnp.einsum("ngab,gh->ngahb", m, eye).reshape(nb, gb * a, gb * b)


def s5_prepare(a_re, a_im, log_dt, b_re, b_im, c_re, c_im):
    g, p = a_re.shape
    gc = b_re.shape[-1]
    rep = lambda v: jnp.repeat(v, gc, axis=-1)
    shp = jax.ShapeDtypeStruct((g, p * gc), F32)
    l_re, l_im, bb_re, bb_im = pl.pallas_call(
        _s5_discretize_kernel, out_shape=(shp, shp, shp, shp), name="s5_discretize",
    )(rep(a_re), rep(a_im), rep(jnp.broadcast_to(log_dt[:, None], (g, p))),
      b_re.reshape(g, p * gc), b_im.reshape(g, p * gc))
    nb = g // SSM_GROUP_BLOCK
    lam_re = l_re[:, ::gc].reshape(nb, 1, SSM_GROUP_BLOCK * p)
    lam_im = l_im[:, ::gc].reshape(nb, 1, SSM_GROUP_BLOCK * p)

    def in_mat(bb):
        m = bb.reshape(g, p, gc).transpose(0, 2, 1).reshape(nb, SSM_GROUP_BLOCK, gc, p)
        return _block_diag(m)

    def out_mat(c):
        m = c.transpose(0, 2, 1).reshape(nb, SSM_GROUP_BLOCK, p, gc)
        return _block_diag(m)

    w_in = jnp.concatenate([in_mat(bb_re), in_mat(bb_im)], axis=-1).astype(BF16)
    w_out = jnp.concatenate([out_mat(c_re), out_mat(-c_im)], axis=1).astype(BF16)
    return lam_re, lam_im, w_in, w_out


def _s5_kernel(u_ref, win_ref, lre_ref, lim_ref, wout_ref, d_ref, o_ref,
               lhs_scr, st_scr, bu_scr, y_scr, *, nb, ch, half):
    k = pl.program_id(1)
    rows = nb * ch
    nslab = lhs_scr.shape[0]

    @pl.when(k == 0)
    def _():
        st_scr[...] = jnp.zeros_like(st_scr)

    for b in range(nb):
        for s in range(nslab):
            lhs_scr[s, pl.ds(b, ch, stride=nb), :] = u_ref[b, :, s * LANES:(s + 1) * LANES]
    u_tm = jnp.concatenate([lhs_scr[s] for s in range(nslab)], axis=1)
    bu_scr[...] = jnp.dot(u_tm.astype(BF16), win_ref[0], preferred_element_type=F32)

    l_re = jnp.broadcast_to(lre_ref[0], (nb, half))
    l_im = jnp.broadcast_to(lim_ref[0], (nb, half))

    def step(t, carry):
        s_re, s_im = carry
        r0 = pl.multiple_of(t * nb, nb)
        n_re = l_re * s_re - l_im * s_im + bu_scr[pl.ds(r0, nb), 0:half]
        n_im = l_re * s_im + l_im * s_re + bu_scr[pl.ds(r0, nb), half:2 * half]
        bu_scr[pl.ds(r0, nb), 0:half] = n_re
        bu_scr[pl.ds(r0, nb), half:2 * half] = n_im
        return n_re, n_im

    s_re, s_im = lax.fori_loop(0, ch, step, (st_scr[:, 0:half], st_scr[:, half:2 * half]), unroll=2)
    st_scr[:, 0:half] = s_re
    st_scr[:, half:2 * half] = s_im

    y = jnp.dot(bu_scr[...].astype(BF16), wout_ref[0], preferred_element_type=F32)
    y = jax.nn.gelu(y + d_ref[...] * u_tm)
    for s in range(nslab):
        y_scr[s] = y[:, s * LANES:(s + 1) * LANES]
    for b in range(nb):
        for s in range(nslab):
            o_ref[b, :, s * LANES:(s + 1) * LANES] = (
                y_scr[s, pl.ds(b, ch, stride=nb), :].astype(o_ref.dtype))


def s5_core(u3, lam_re, lam_im, w_in, w_out, d_skip):
    nb, length, d = u3.shape
    assert nb == SUBLANES, "the scan keeps one batch row per sublane"
    cb = w_in.shape[1]
    half = w_in.shape[2] // 2
    ch = SSM_CHUNK
    kern = functools.partial(_s5_kernel, nb=nb, ch=ch, half=half)
    return pl.pallas_call(
        kern,
        grid=(d // cb, length // ch),
        in_specs=[
            pl.BlockSpec((nb, ch, cb), lambda g, k: (0, k, g)),
            pl.BlockSpec((1, cb, 2 * half), lambda g, k: (g, 0, 0)),
            pl.BlockSpec((1, 1, half), lambda g, k: (g, 0, 0)),
            pl.BlockSpec((1, 1, half), lambda g, k: (g, 0, 0)),
            pl.BlockSpec((1, 2 * half, cb), lambda g, k: (g, 0, 0)),
            pl.BlockSpec((1, cb), lambda g, k: (0, g)),
        ],
        out_specs=pl.BlockSpec((nb, ch, cb), lambda g, k: (0, k, g)),
        out_shape=jax.ShapeDtypeStruct((nb, length, d), BF16),
        scratch_shapes=[
            pltpu.VMEM((cb // LANES, nb * ch, LANES), F32),
            pltpu.VMEM((nb, 2 * half), F32),
            pltpu.VMEM((nb * ch, 2 * half), F32),
            pltpu.VMEM((cb // LANES, nb * ch, LANES), F32),
        ],
        compiler_params=_params(("parallel", "arbitrary")),
        name="s5_core",
    )(u3, w_in, lam_re, lam_im, w_out, d_skip.reshape(1, d))


def _glu_kernel(g_ref, wv_ref, wg_ref, bv_ref, bg_ref, x_ref, o_ref):
    a = g_ref[...]
    val = jnp.dot(a, wv_ref[...], preferred_element_type=F32) + bv_ref[...]
    gate = jnp.dot(a, wg_ref[...], preferred_element_type=F32) + bg_ref[...]
    o_ref[...] = x_ref[...] + val * jax.nn.sigmoid(gate)


def glu_residual(g, w_glu, b_glu, x, tm=1024, tn=512):
    t, d = x.shape
    nj = d // tn
    b2 = b_glu.reshape(1, 2 * d)
    return pl.pallas_call(
        _glu_kernel,
        grid=(t // tm, nj),
        in_specs=[
            pl.BlockSpec((tm, d), lambda i, j: (i, 0)),
            pl.BlockSpec((d, tn), lambda i, j: (0, j)),
            pl.BlockSpec((d, tn), lambda i, j: (0, j + nj)),
            pl.BlockSpec((1, tn), lambda i, j: (0, j)),
            pl.BlockSpec((1, tn), lambda i, j: (0, j + nj)),
            pl.BlockSpec((tm, tn), lambda i, j: (i, j)),
        ],
        out_specs=pl.BlockSpec((tm, tn), lambda i, j: (i, j)),
        out_shape=jax.ShapeDtypeStruct((t, d), F32),
        compiler_params=_params(("parallel", "arbitrary")),
        name="glu_residual",
    )(g, w_glu, w_glu, b2, b2, x)


def _token_rows(d):
    return d // (2 * LANES)


def _store_token_major(ref, row0, x):
    n, d = x.shape
    nrow = _token_rows(d)
    words = pltpu.pack_elementwise([x[:, :d // 2], x[:, d // 2:]], packed_dtype=BF16)
    for c in range(nrow):
        ref[pl.ds(row0 + c, n, stride=nrow), :] = words[:, c * LANES:(c + 1) * LANES]


def _load_token_major(ref, row0, n, d):
    nrow = _token_rows(d)
    words = [ref[pl.ds(row0 + c, n, stride=nrow), :] for c in range(nrow)]
    halves = [[pltpu.unpack_elementwise(w, index=i, packed_dtype=BF16, unpacked_dtype=F32) for w in words]
              for i in (0, 1)]
    return jnp.concatenate(halves[0] + halves[1], axis=1)


def _router_kernel(h_ref, g_ref, wr_ref, br_ref, u_ref, meta_ref, cnt_ref, base_scr, *, tm):
    i = pl.program_id(0)

    @pl.when(i == 0)
    def _():
        base_scr[...] = jnp.zeros_like(base_scr)

    x = h_ref[...]
    u = x * _rms_scale(x) * g_ref[...]
    _store_token_major(u_ref, 0, u)
    logits = jnp.dot(u, wr_ref[...], preferred_element_type=F32,
                     precision=lax.Precision.HIGHEST) + br_ref[...]
    lane = lax.broadcasted_iota(I32, logits.shape, 1)
    big = jnp.int32(1 << 20)
    is_grp = (lane >= N_EXPERTS) & (lane < N_EXPERTS + N_GROUPS)
    gl = jnp.where(is_grp, logits, -jnp.inf)
    gmax = jnp.max(gl, axis=-1, keepdims=True)
    gidx = jnp.min(jnp.where(gl == gmax, lane, big), axis=-1, keepdims=True) - N_EXPERTS
    gsum = jnp.sum(jnp.where(is_grp, jnp.exp(logits - gmax), 0.0), axis=-1, keepdims=True)
    grp_w = 1.0 / gsum
    in_grp = (lane < N_EXPERTS) & (jnp.right_shift(lane, EXPERTS_PER_GROUP.bit_length() - 1) == gidx)
    sel = jnp.where(in_grp, logits, -jnp.inf)
    v1 = jnp.max(sel, axis=-1, keepdims=True)
    i1 = jnp.min(jnp.where(sel == v1, lane, big), axis=-1, keepdims=True)
    sel2 = jnp.where(lane == i1, -jnp.inf, sel)
    v2 = jnp.max(sel2, axis=-1, keepdims=True)
    i2 = jnp.min(jnp.where(sel2 == v2, lane, big), axis=-1, keepdims=True)
    e21 = jnp.exp(v2 - v1)
    w1 = grp_w / (1.0 + e21)
    w2 = grp_w * e21 / (1.0 + e21)
    oh1 = lane == i1
    oh2 = lane == i2
    oh = (oh1 | oh2).astype(F32)
    rr = lax.broadcasted_iota(I32, (tm, tm), 0)
    cc = lax.broadcasted_iota(I32, (tm, tm), 1)
    tri = (cc < rr).astype(BF16)
    before = jnp.dot(tri, oh.astype(BF16), preferred_element_type=F32) + base_scr[...]
    r1 = jnp.sum(jnp.where(oh1, before, 0.0), axis=-1, keepdims=True)
    r2 = jnp.sum(jnp.where(oh2, before, 0.0), axis=-1, keepdims=True)
    base_scr[...] = base_scr[...] + jnp.sum(oh, axis=0, keepdims=True)
    cnt_ref[...] = base_scr[...]
    meta = jnp.where(lane == 0, i1.astype(F32), 0.0)
    meta = jnp.where(lane == 1, i2.astype(F32), meta)
    meta = jnp.where(lane == 2, w1, meta)
    meta = jnp.where(lane == 3, w2, meta)
    meta = jnp.where(lane == 4, r1, meta)
    meta = jnp.where(lane == 5, r2, meta)
    meta_ref[...] = meta


def moe_router(h, norm_g, r_grp_w, r_grp_b, r_exp_w, r_exp_b):
    t, d = h.shape
    tm = ROUTER_TILE
    pad = LANES - N_EXPERTS - N_GROUPS
    wr = jnp.concatenate([r_exp_w, r_grp_w, jnp.zeros((d, pad), F32)], axis=1)
    br = jnp.concatenate([r_exp_b, r_grp_b, jnp.zeros((pad,), F32)]).reshape(1, LANES)
    return pl.pallas_call(
        functools.partial(_router_kernel, tm=tm),
        grid=(t // tm,),
        in_specs=[
            pl.BlockSpec((tm, d), lambda i: (i, 0)),
            pl.BlockSpec((1, d), lambda i: (0, 0)),
            pl.BlockSpec((d, LANES), lambda i: (0, 0)),
            pl.BlockSpec((1, LANES), lambda i: (0, 0)),
        ],
        out_specs=[
            pl.BlockSpec((tm * _token_rows(d), LANES), lambda i: (i, 0)),
            pl.BlockSpec((tm, LANES), lambda i: (i, 0)),
            pl.BlockSpec((1, LANES), lambda i: (0, 0)),
        ],
        out_shape=[
            jax.ShapeDtypeStruct((t * _token_rows(d), LANES), PACKED),
            jax.ShapeDtypeStruct((t, LANES), F32),
            jax.ShapeDtypeStruct((1, LANES), F32),
        ],
        scratch_shapes=[pltpu.VMEM((1, LANES), F32)],
        compiler_params=_params(("arbitrary",)),
        name="moe_router",
    )(h, norm_g.reshape(1, d), wr, br)


def moe_plan(meta, counts, t):
    tile = MOE_TILE
    n_tiles = (2 * t) // tile + N_EXPERTS
    cnt = counts[0, :N_EXPERTS].astype(I32)
    padded = ((cnt + tile - 1) // tile) * tile
    ends = jnp.cumsum(padded)
    offs = ends - padded
    e = meta[:, 0:2].astype(I32)
    rank = meta[:, 4:6].astype(I32)
    slot = offs[e] + rank
    row = jnp.arange(t, dtype=I32)[:, None] + jnp.array([0, t], I32)[None, :]
    row_of_slot = jnp.full((n_tiles * tile,), -1, I32).at[slot.reshape(-1)].set(row.reshape(-1))
    row_of_slot = row_of_slot.reshape(n_tiles, tile)
    valid = row_of_slot >= 0
    src = jnp.where(valid, jnp.where(row_of_slot >= t, row_of_slot - t, row_of_slot), 0)
    spare = 2 * t + (jnp.arange(n_tiles, dtype=I32)[:, None] % 2) * tile + jnp.arange(tile, dtype=I32)[None, :]
    dst = jnp.where(valid, row_of_slot, spare)
    slot_io = jnp.concatenate([src, dst], axis=1).reshape(n_tiles, 1, 2 * tile)
    starts = jnp.arange(n_tiles, dtype=I32) * tile
    tile_expert = jnp.minimum(
        jnp.sum((starts[:, None] >= ends[None, :]).astype(I32), axis=1), N_EXPERTS - 1)
    n_used = (ends[-1] // tile).astype(I32).reshape(1)
    return slot_io, tile_expert, n_used


def _moe_expert_kernel(te_ref, nu_ref, cur_ref, nxt_ref, u_hbm, wg_ref, wu_ref, wd_ref, o_hbm,
                       idx0, idx1, rows, yrows, wg_s, wu_s, wd_s, sem_i, sem_g, sem_s, *, tile, d, t):
    i = pl.program_id(0)
    n_used = nu_ref[0]
    nchunk = _token_rows(d)
    buf_rows = tile * nchunk
    cur = i % 2
    nxt = 1 - cur
    idx_smem = (idx0, idx1)

    def by_parity(par, fn):
        for b in (0, 1):
            pl.when(par == b)(functools.partial(fn, b))

    def idx_copy(src_ref, b):
        return pltpu.make_async_copy(src_ref.at[0, 0], idx_smem[b], sem_i.at[b])

    def gather_copy(tok, r, b):
        src = u_hbm.at[pl.ds(pl.multiple_of(tok * nchunk, nchunk), nchunk)]
        dst = rows.at[pl.ds(pl.multiple_of(b * buf_rows + r * nchunk, nchunk), nchunk)]
        return pltpu.make_async_copy(src, dst, sem_g.at[b])

    def scatter_copy(row, r, b):
        src = yrows.at[pl.ds(pl.multiple_of(b * buf_rows + r * nchunk, nchunk), nchunk)]
        dst = o_hbm.at[pl.ds(pl.multiple_of(row * nchunk, nchunk), nchunk)]
        return pltpu.make_async_copy(src, dst, sem_s.at[b])

    def for_rows(fn):
        def body(r, c):
            fn(r)
            return c
        lax.fori_loop(0, tile, body, 0, unroll=8)

    def start_gather(b):
        for_rows(lambda r: gather_copy(idx_smem[b][r], r, b).start())

    def wait_gather(b):
        for_rows(lambda r: gather_copy(0, r, b).wait())

    def start_scatter(b):
        for_rows(lambda r: scatter_copy(idx_smem[b][tile + r], r, b).start())

    def wait_scatter(b):
        for_rows(lambda r: scatter_copy(0, r, b).wait())

    @pl.when(i < n_used)
    def _():
        @pl.when(i == 0)
        def _():
            first = idx_copy(cur_ref, 0)
            first.start()
            first.wait()
            start_gather(0)
            yrows[pl.ds(0, buf_rows), :] = jnp.zeros((buf_rows, LANES), PACKED)
            fills = [pltpu.make_async_copy(yrows.at[pl.ds(0, buf_rows)],
                                           o_hbm.at[pl.ds((2 * t + b * tile) * nchunk, buf_rows)], sem_s.at[b])
                     for b in (0, 1)]
            for cp in fills:
                cp.start()
            for cp in fills:
                cp.wait()

        has_next = i + 1 < n_used

        @pl.when(has_next)
        def _():
            by_parity(nxt, lambda b: idx_copy(nxt_ref, b).start())

        by_parity(cur, wait_gather)

        prev = te_ref[jnp.maximum(i - 1, 0)]

        @pl.when((i == 0) | (te_ref[i] != prev))
        def _():
            wg_s[...] = wg_ref[0, 0].astype(BF16)
            wu_s[...] = wu_ref[0, 0].astype(BF16)
            wd_s[...] = wd_ref[0, 0].astype(BF16)

        @pl.when(has_next)
        def _():
            def prefetch(b):
                idx_copy(nxt_ref, b).wait()
                start_gather(b)
            by_parity(nxt, prefetch)

        x = _load_token_major(rows, pl.multiple_of(cur * buf_rows, buf_rows), tile, d).astype(BF16)
        gate = jnp.dot(x, wg_s[...], preferred_element_type=F32)
        up = jnp.dot(x, wu_s[...], preferred_element_type=F32)
        hid = (jax.nn.silu(gate) * up).astype(BF16)
        y = jnp.dot(hid, wd_s[...], preferred_element_type=F32)

        @pl.when(i >= 2)
        def _():
            by_parity(cur, wait_scatter)

        _store_token_major(yrows, pl.multiple_of(cur * buf_rows, buf_rows), y)
        by_parity(cur, start_scatter)

        @pl.when(i == n_used - 1)
        def _():
            @pl.when(i >= 1)
            def _():
                by_parity(nxt, wait_scatter)

            by_parity(cur, wait_scatter)


def moe_experts(u_rows, slot_io, tile_expert, n_used, w_gate, w_up, w_down, layer, t, d):
    n_tiles, _, two_tile = slot_io.shape
    tile = two_tile // 2
    hdim = w_gate.shape[-1]
    nchunk = _token_rows(d)
    last = n_tiles - 1
    grid_spec = pltpu.PrefetchScalarGridSpec(
        num_scalar_prefetch=2,
        grid=(n_tiles,),
        in_specs=[
            pl.BlockSpec((1, 1, two_tile), lambda i, te, nu: (i, 0, 0)),
            pl.BlockSpec((1, 1, two_tile), lambda i, te, nu: (jnp.minimum(i + 1, last), 0, 0)),
            pl.BlockSpec(memory_space=pl.ANY),
            pl.BlockSpec((1, 1, d, hdim), lambda i, te, nu: (layer, te[i], 0, 0)),
            pl.BlockSpec((1, 1, d, hdim), lambda i, te, nu: (layer, te[i], 0, 0)),
            pl.BlockSpec((1, 1, hdim, d), lambda i, te, nu: (layer, te[i], 0, 0)),
        ],
        out_specs=pl.BlockSpec(memory_space=pl.ANY),
        scratch_shapes=[
            pltpu.SMEM((two_tile,), I32),
            pltpu.SMEM((two_tile,), I32),
            pltpu.VMEM((2 * tile * nchunk, LANES), PACKED),
            pltpu.VMEM((2 * tile * nchunk, LANES), PACKED),
            pltpu.VMEM((d, hdim), BF16),
            pltpu.VMEM((d, hdim), BF16),
            pltpu.VMEM((hdim, d), BF16),
            pltpu.SemaphoreType.DMA((2,)),
            pltpu.SemaphoreType.DMA((2,)),
            pltpu.SemaphoreType.DMA((2,)),
        ],
    )
    return pl.pallas_call(
        functools.partial(_moe_expert_kernel, tile=tile, d=d, t=t),
        grid_spec=grid_spec,
        out_shape=jax.ShapeDtypeStruct(((2 * t + 2 * tile) * nchunk, LANES), PACKED),
        compiler_params=_params(("arbitrary",)),
        name="moe_experts",
    )(tile_expert, n_used, slot_io, slot_io, u_rows, w_gate, w_up, w_down)


def _combine_ple_kernel(y1_ref, y2_ref, h_ref, meta_ref, p_ref, g_ref, wgate_ref, wproj_ref, fg_ref,
                        o_ref, *, tm, final_norm):
    d = h_ref.shape[1]
    meta = meta_ref[...]
    y1 = _load_token_major(y1_ref, 0, tm, d)
    y2 = _load_token_major(y2_ref, 0, tm, d)
    h2 = h_ref[...] + (meta[:, 2:3] * y1 + meta[:, 3:4] * y2)
    a = (h2 * _rms_scale(h2) * g_ref[...]).astype(BF16)
    gate = jax.nn.sigmoid(jnp.dot(a, wgate_ref[...], preferred_element_type=F32))
    proj = jnp.dot(p_ref[...].astype(BF16), wproj_ref[...], preferred_element_type=F32)
    h3 = h2 + gate * proj
    if final_norm:
        h3 = h3 * _rms_scale(h3) * fg_ref[...]
    o_ref[...] = h3


def moe_combine_ple(y_rows, h, meta, p_all, layer, ple_norm_g, ple_gate_w, ple_proj_w, final_g, final_norm,
                    tm=256):
    t, d = h.shape
    pd = p_all.shape[-1]
    nchunk = _token_rows(d)
    nt = t // tm
    return pl.pallas_call(
        functools.partial(_combine_ple_kernel, tm=tm, final_norm=final_norm),
        grid=(nt,),
        in_specs=[
            pl.BlockSpec((tm * nchunk, LANES), lambda i: (i, 0)),
            pl.BlockSpec((tm * nchunk, LANES), lambda i: (nt + i, 0)),
            pl.BlockSpec((tm, d), lambda i: (i, 0)),
            pl.BlockSpec((tm, LANES), lambda i: (i, 0)),
            pl.BlockSpec((tm, pd), lambda i: (layer * nt + i, 0)),
            pl.BlockSpec((1, d), lambda i: (0, 0)),
            pl.BlockSpec((d, d), lambda i: (0, 0)),
            pl.BlockSpec((pd, d), lambda i: (0, 0)),
            pl.BlockSpec((1, d), lambda i: (0, 0)),
        ],
        out_specs=pl.BlockSpec((tm, d), lambda i: (i, 0)),
        out_shape=jax.ShapeDtypeStruct((t, d), F32),
        compiler_params=_params(("parallel",)),
        name="moe_combine_ple",
    )(y_rows, y_rows, h, meta, p_all, ple_norm_g.reshape(1, d), ple_gate_w, ple_proj_w, final_g.reshape(1, d))


def moe_ple_layer(h, p_all, layer, ffn_norm_g, r_grp_w, r_grp_b, r_exp_w, r_exp_b, w_gate, w_up, w_down,
                  ple_norm_g, ple_gate_w, ple_proj_w, final_g, final_norm):
    t, d = h.shape
    u_rows, meta, counts = moe_router(h, ffn_norm_g, r_grp_w, r_grp_b, r_exp_w, r_exp_b)
    slot_io, tile_expert, n_used = moe_plan(meta, counts, t)
    y_rows = moe_experts(u_rows, slot_io, tile_expert, n_used, w_gate, w_up, w_down, layer, t, d)
    return moe_combine_ple(y_rows, h, meta, p_all, layer, ple_norm_g, ple_gate_w, ple_proj_w, final_g,
                           final_norm)


def _qkv_kernel(x_ref, gq_ref, gkv_ref, w_ref, cos_ref, sin_ref, o_ref, act_scr, *, nq_tiles, scale):
    j = pl.program_id(1)

    @pl.when(j == 0)
    def _():
        x = x_ref[...]
        xn = x * _rms_scale(x)
        act_scr[0] = (xn * gq_ref[...]).astype(BF16)
        act_scr[1] = (xn * gkv_ref[...]).astype(BF16)

    a = act_scr[(j >= nq_tiles).astype(I32)]
    acc = jnp.dot(a, w_ref[...], preferred_element_type=F32)
    tn = acc.shape[1]
    reps = tn // HEAD_DIM
    cos = jnp.tile(cos_ref[...], (1, reps))
    sin = jnp.tile(sin_ref[...], (1, reps))
    lane = lax.broadcasted_iota(I32, acc.shape, 1)
    first_half = (lane & (HEAD_DIM - 1)) < (HEAD_DIM // 2)
    partner = jnp.where(first_half, pltpu.roll(acc, tn - HEAD_DIM // 2, 1), pltpu.roll(acc, HEAD_DIM // 2, 1))
    roped = acc * cos + partner * sin
    out = jnp.where(j <= nq_tiles, roped, acc)
    out = out * jnp.where(j < nq_tiles, scale, 1.0)
    o_ref[...] = out.astype(o_ref.dtype)


def qkv_project(h, attn_g, kv_g, w_all, cos_f, sin_f, length, tm=512, tn=1024):
    t, d = h.shape
    n_all = w_all.shape[1]
    kvw = N_KV_HEADS * HEAD_DIM
    assert tn == kvw
    nq_tiles = (n_all - 2 * kvw) // tn
    nl = length // tm
    return pl.pallas_call(
        functools.partial(_qkv_kernel, nq_tiles=nq_tiles, scale=HEAD_DIM ** -0.5),
        grid=(t // tm, n_all // tn),
        in_specs=[
            pl.BlockSpec((tm, d), lambda i, j: (i, 0)),
            pl.BlockSpec((1, d), lambda i, j: (0, 0)),
            pl.BlockSpec((1, d), lambda i, j: (0, 0)),
            pl.BlockSpec((d, tn), lambda i, j: (0, j)),
            pl.BlockSpec((tm, HEAD_DIM), lambda i, j: (i % nl, 0)),
            pl.BlockSpec((tm, HEAD_DIM), lambda i, j: (i % nl, 0)),
        ],
        out_specs=pl.BlockSpec((tm, tn), lambda i, j: (i, j)),
        out_shape=jax.ShapeDtypeStruct((t, n_all), BF16),
        scratch_shapes=[pltpu.VMEM((2, tm, d), BF16)],
        compiler_params=_params(("parallel", "arbitrary")),
        name="qkv_project",
    )(h, attn_g.reshape(1, d), kv_g.reshape(1, d), w_all, cos_f, sin_f)


def _attn_kernel(q0_ref, q1_ref, q2_ref, k_ref, v_ref, o_ref, q_scr, k_scr, v_scr, o_scr, lse_scr, bias_scr,
                 stage_scr, ostage_scr, *, length, patterns, unroll):
    blk = ATTN_BLOCK
    npat = len(patterns)
    q_refs = (q0_ref, q1_ref, q2_ref)
    for g in range(npat):
        for rep in range(KV_REP):
            q_scr[g * KV_REP + rep] = q_refs[g][:, rep * HEAD_DIM:(rep + 1) * HEAD_DIM].astype(F32)
    k_scr[...] = k_ref[...].astype(F32)
    v_scr[...] = v_ref[...].astype(F32)

    n_steps = patterns[0][0] // patterns[0][1]
    assert all(w // dl == n_steps for w, dl in patterns)
    qi = lax.broadcasted_iota(I32, (blk, 2 * blk), 0)
    kj = lax.broadcasted_iota(I32, (blk, 2 * blk), 1)
    for first in (0, 1):
        dist = qi - kj + first * blk
        bias_scr[first] = jnp.where((dist >= 0) & (dist <= n_steps), 0.0, NEG_INF)

    def attend(q, kk, vv, bias):
        s = lax.dot_general(q, kk, (((1,), (1,)), ((), ())), preferred_element_type=F32) + bias
        m = jnp.max(s, axis=-1, keepdims=True)
        e = jnp.exp(s - m)
        den = jnp.sum(e, axis=-1, keepdims=True)
        pv = jnp.dot(e.astype(BF16), vv, preferred_element_type=F32)
        return pv / den, jnp.broadcast_to(m + jnp.log(den), (blk, HEAD_DIM))

    for g, (window, dil) in enumerate(patterns):
        m_len = length // dil
        nblk = m_len // blk
        win = min(2 * blk, m_len)
        staged = dil > MAX_CHEAP_SUBLANE_STRIDE

        if staged:
            win = 2 * blk
            srcs = [q_scr.at[g * KV_REP + rep] for rep in range(KV_REP)] + [k_scr, v_scr]
            for a, src in enumerate(srcs):
                for res in range(dil):
                    stage_scr[a, res * m_len:(res + 1) * m_len, :] = (
                        src[pl.ds(res, m_len, stride=dil), :].astype(BF16))
                stage_scr[a, length:length + blk, :] = jnp.zeros((blk, HEAD_DIM), BF16)

        def tile(idx, c, g=g, dil=dil, nblk=nblk, win=win, m_len=m_len, staged=staged):
            r = idx // nblk
            n = idx % nblk
            ks = jnp.maximum(n - 1, 0) * blk
            bias = bias_scr[jnp.minimum(n, 1), :, 0:win]
            if staged:
                q_rows = pl.ds(pl.multiple_of(r * m_len + n * blk, blk), blk)
                k_rows = pl.ds(pl.multiple_of(r * m_len + ks, blk), win)
                kk = stage_scr[KV_REP, k_rows, :]
                vv = stage_scr[KV_REP + 1, k_rows, :]
            else:
                q_rows = pl.ds(n * (blk * dil) + r, blk, stride=dil)
                k_rows = pl.ds(ks * dil + r, win, stride=dil)
                kk = k_scr[k_rows, :].astype(BF16)
                vv = v_scr[k_rows, :].astype(BF16)
            for rep in range(KV_REP):
                slot = g * KV_REP + rep
                if staged:
                    o, lse = attend(stage_scr[rep, q_rows, :], kk, vv, bias)
                    ostage_scr[rep, q_rows, :] = o
                    ostage_scr[KV_REP + rep, q_rows, :] = lse
                else:
                    o, lse = attend(q_scr[slot, q_rows, :].astype(BF16), kk, vv, bias)
                    o_scr[slot, q_rows, :] = o
                    lse_scr[slot, q_rows, :] = lse
            return c

        lax.fori_loop(0, dil * nblk, tile, 0, unroll=unroll)

        if staged:
            for rep in range(KV_REP):
                slot = g * KV_REP + rep
                for res in range(dil):
                    rows = pl.ds(res, m_len, stride=dil)
                    o_scr[slot, rows, :] = ostage_scr[rep, res * m_len:(res + 1) * m_len, :]
                    lse_scr[slot, rows, :] = ostage_scr[KV_REP + rep, res * m_len:(res + 1) * m_len, :]

    for rep in range(KV_REP):
        lses = [lse_scr[g * KV_REP + rep] for g in range(npat)]
        top = functools.reduce(jnp.maximum, lses)
        ws = [jnp.exp(l - top) for l in lses]
        num = sum(w * o_scr[g * KV_REP + rep] for g, w in enumerate(ws))
        o_ref[:, rep * HEAD_DIM:(rep + 1) * HEAD_DIM] = (num / sum(ws)).astype(o_ref.dtype)


def dilated_attention(qkv, bsz, length):
    t = qkv.shape[0]
    qw = KV_REP * HEAD_DIM
    npat = len(DILATED_PATTERNS)
    k_base = npat * N_Q_HEADS
    v_base = k_base + N_KV_HEADS
    for _, dil in DILATED_PATTERNS:
        assert (length // dil) % ATTN_BLOCK == 0

    def q_spec(g):
        return pl.BlockSpec((length, qw), lambda b, h: (b, g * N_KV_HEADS + h))

    return pl.pallas_call(
        functools.partial(_attn_kernel, length=length, patterns=DILATED_PATTERNS, unroll=4),
        grid=(bsz, N_KV_HEADS),
        in_specs=[
            q_spec(0), q_spec(1), q_spec(2),
            pl.BlockSpec((length, HEAD_DIM), lambda b, h: (b, k_base + h)),
            pl.BlockSpec((length, HEAD_DIM), lambda b, h: (b, v_base + h)),
        ],
        out_specs=pl.BlockSpec((length, qw), lambda b, h: (b, h)),
        out_shape=jax.ShapeDtypeStruct((t, N_Q_HEADS * HEAD_DIM), BF16),
        scratch_shapes=[
            pltpu.VMEM((npat * KV_REP, length, HEAD_DIM), F32),
            pltpu.VMEM((length, HEAD_DIM), F32),
            pltpu.VMEM((length, HEAD_DIM), F32),
            pltpu.VMEM((npat * KV_REP, length, HEAD_DIM), F32),
            pltpu.VMEM((npat * KV_REP, length, HEAD_DIM), F32),
            pltpu.VMEM((2, ATTN_BLOCK, 2 * ATTN_BLOCK), F32),
            pltpu.VMEM((KV_REP + 2, length + ATTN_BLOCK, HEAD_DIM), BF16),
            pltpu.VMEM((2 * KV_REP, length, HEAD_DIM), F32),
        ],
        compiler_params=_params(("parallel", "parallel")),
        name="dilated_attention",
    )(qkv, qkv, qkv, qkv, qkv)


def _proj_residual_kernel(a_ref, w_ref, x_ref, o_ref):
    o_ref[...] = x_ref[...] + jnp.dot(a_ref[...], w_ref[...], preferred_element_type=F32)


def proj_residual(a, w, x, tm=1024, tn=512):
    t, d = x.shape
    kdim = a.shape[1]
    return pl.pallas_call(
        _proj_residual_kernel,
        grid=(t // tm, d // tn),
        in_specs=[
            pl.BlockSpec((tm, kdim), lambda i, j: (i, 0)),
            pl.BlockSpec((kdim, tn), lambda i, j: (0, j)),
            pl.BlockSpec((tm, tn), lambda i, j: (i, j)),
        ],
        out_specs=pl.BlockSpec((tm, tn), lambda i, j: (i, j)),
        out_shape=jax.ShapeDtypeStruct((t, d), F32),
        compiler_params=_params(("parallel", "arbitrary")),
        name="proj_residual",
    )(a, w, x)


def _rope_tables(length):
    inv = 1.0 / (ROPE_THETA ** (jnp.arange(0, HEAD_DIM, 2, dtype=F32) / HEAD_DIM))
    ang = jnp.arange(length, dtype=F32)[:, None] * inv[None, :]
    cos, sin = jnp.cos(ang), jnp.sin(ang)
    return jnp.concatenate([cos, cos], axis=-1), jnp.concatenate([-sin, sin], axis=-1)


def kernel(x, p, s5_norm_g, s5_a_re, s5_a_im, s5_log_dt, s5_b_re, s5_b_im, s5_c_re, s5_c_im, s5_d, s5_w_glu, s5_b_glu, kv_norm_g, w_k, w_v, attn_norm_g, w_q, w_o, ffn_norm_g, router_grp_w, router_grp_b, router_exp_w, router_exp_b, w_gate, w_up, w_down, ple_norm_g, ple_gate_w, ple_proj_w, final_norm_g):
    bsz, length, d = x.shape
    depth = p.shape[0]
    n_s5 = s5_norm_g.shape[0]
    t = bsz * length
    h = x.reshape(t, d)
    cos_f, sin_f = _rope_tables(length)
    for i in range(depth):
        if i < n_s5:
            lam_re, lam_im, w_in, w_out = s5_prepare(
                s5_a_re[i], s5_a_im[i], s5_log_dt[i], s5_b_re[i], s5_b_im[i], s5_c_re[i], s5_c_im[i])
            u = rmsnorm(h, s5_norm_g[i])
            g = s5_core(u.reshape(bsz, length, d), lam_re, lam_im, w_in, w_out, s5_d[i])
            h = glu_residual(g.reshape(t, d), s5_w_glu[i].astype(BF16), s5_b_glu[i], h)
        else:
            j = i - n_s5
            assert j == 0, "shared K/V reuse across several attention layers is not implemented"
            w_all = jnp.concatenate([w_q[j], w_k, w_v], axis=1).astype(BF16)
            qkv = qkv_project(h, attn_norm_g[j], kv_norm_g, w_all, cos_f, sin_f, length)
            attn = dilated_attention(qkv, bsz, length)
            h = proj_residual(attn, w_o[j].astype(BF16), h)
        h = moe_ple_layer(
            h, p.reshape(depth * t, -1), i, ffn_norm_g[i], router_grp_w[i], router_grp_b[i], router_exp_w[i],
            router_exp_b[i], w_gate, w_up, w_down, ple_norm_g[i], ple_gate_w[i].astype(BF16),
            ple_proj_w[i].astype(BF16), final_norm_g, final_norm=(i == depth - 1))
    return h.reshape(bsz, length, d)
```

```python
import functools

import jax
import jax.numpy as jnp
from jax import lax
from jax.experimental import pallas as pl
from jax.experimental.pallas import tpu as pltpu

F32 = jnp.float32
BF16 = jnp.bfloat16
I32 = jnp.int32
PACKED = jnp.uint32

NORM_EPS = 1e-6
LANES = 128
SUBLANES = 8
MXU_DIM = 256
VMEM_LIMIT_BYTES = 56 * 1024 * 1024

SSM_GROUP = 16
SSM_STATE = 64
SSM_GROUP_BLOCK = MXU_DIM // SSM_GROUP
SSM_CHUNK = 128
HEAD_DIM = 128
N_Q_HEADS = 16
N_KV_HEADS = 8
KV_REP = N_Q_HEADS // N_KV_HEADS
DILATED_PATTERNS = ((128, 1), (512, 4), (2048, 16))
ATTN_BLOCK = 128
MAX_CHEAP_SUBLANE_STRIDE = 4
ROPE_THETA = 10000.0
NEG_INF = -1e30
N_GROUPS = 4
EXPERTS_PER_GROUP = 8
N_EXPERTS = N_GROUPS * EXPERTS_PER_GROUP
MOE_TILE = 256
ROUTER_TILE = 256


def _params(sem):
    return pltpu.CompilerParams(dimension_semantics=sem, vmem_limit_bytes=VMEM_LIMIT_BYTES)


def _rms_scale(x):
    return lax.rsqrt(jnp.mean(x * x, axis=-1, keepdims=True) + NORM_EPS)


def _rmsnorm_kernel(x_ref, g_ref, o_ref):
    x = x_ref[...]
    o_ref[...] = (x * _rms_scale(x) * g_ref[...]).astype(o_ref.dtype)


def rmsnorm(x, g, out_dtype=F32, tm=512):
    t, d = x.shape
    return pl.pallas_call(
        _rmsnorm_kernel,
        grid=(t // tm,),
        in_specs=[pl.BlockSpec((tm, d), lambda i: (i, 0)),
                  pl.BlockSpec((1, d), lambda i: (0, 0))],
        out_specs=pl.BlockSpec((tm, d), lambda i: (i, 0)),
        out_shape=jax.ShapeDtypeStruct((t, d), out_dtype),
        compiler_params=_params(("parallel",)),
        name="rmsnorm",
    )(x, g.reshape(1, d))


def _s5_discretize_kernel(are_ref, aim_ref, ldt_ref, bre_ref, bim_ref,
                          lre_ref, lim_ref, bbre_ref, bbim_ref):
    a_re, a_im = are_ref[...], aim_ref[...]
    dt = jnp.exp(ldt_ref[...])
    mag = jnp.exp(dt * a_re)
    l_re = mag * jnp.cos(dt * a_im)
    l_im = mag * jnp.sin(dt * a_im)
    n_re, n_im = l_re - 1.0, l_im
    den = a_re * a_re + a_im * a_im
    f_re = (n_re * a_re + n_im * a_im) / den
    f_im = (n_im * a_re - n_re * a_im) / den
    b_re, b_im = bre_ref[...], bim_ref[...]
    lre_ref[...] = l_re
    lim_ref[...] = l_im
    bbre_ref[...] = f_re * b_re - f_im * b_im
    bbim_ref[...] = f_re * b_im + f_im * b_re


def _block_diag(m):
    nb, gb, a, b = m.shape
    eye = jnp.eye(gb, dtype=m.dtype)
    return jnp.einsum("ngab,gh->ngahb", m, eye).reshape(nb, gb * a, gb * b)


def s5_prepare(a_re, a_im, log_dt, b_re, b_im, c_re, c_im):
    g, p = a_re.shape
    gc = b_re.shape[-1]
    rep = lambda v: jnp.repeat(v, gc, axis=-1)
    shp = jax.ShapeDtypeStruct((g, p * gc), F32)
    l_re, l_im, bb_re, bb_im = pl.pallas_call(
        _s5_discretize_kernel, out_shape=(shp, shp, shp, shp), name="s5_discretize",
    )(rep(a_re), rep(a_im), rep(jnp.broadcast_to(log_dt[:, None], (g, p))),
      b_re.reshape(g, p * gc), b_im.reshape(g, p * gc))
    nb = g // SSM_GROUP_BLOCK
    lam_re = l_re[:, ::gc].reshape(nb, 1, SSM_GROUP_BLOCK * p)
    lam_im = l_im[:, ::gc].reshape(nb, 1, SSM_GROUP_BLOCK * p)

    def in_mat(bb):
        m = bb.reshape(g, p, gc).transpose(0, 2, 1).reshape(nb, SSM_GROUP_BLOCK, gc, p)
        return _block_diag(m)

    def out_mat(c):
        m = c.transpose(0, 2, 1).reshape(nb, SSM_GROUP_BLOCK, p, gc)
        return _block_diag(m)

    w_in = jnp.concatenate([in_mat(bb_re), in_mat(bb_im)], axis=-1).astype(BF16)
    w_out = jnp.concatenate([out_mat(c_re), out_mat(-c_im)], axis=1).astype(BF16)
    return lam_re, lam_im, w_in, w_out


def _s5_kernel(u_ref, win_ref, lre_ref, lim_ref, wout_ref, d_ref, o_ref,
               lhs_scr, st_scr, bu_scr, y_scr, *, nb, ch, half):
    k = pl.program_id(1)
    rows = nb * ch
    nslab = lhs_scr.shape[0]

    @pl.when(k == 0)
    def _():
        st_scr[...] = jnp.zeros_like(st_scr)

    for b in range(nb):
        for s in range(nslab):
            lhs_scr[s, pl.ds(b, ch, stride=nb), :] = u_ref[b, :, s * LANES:(s + 1) * LANES]
    u_tm = jnp.concatenate([lhs_scr[s] for s in range(nslab)], axis=1)
    bu_scr[...] = jnp.dot(u_tm.astype(BF16), win_ref[0], preferred_element_type=F32)

    l_re = jnp.broadcast_to(lre_ref[0], (nb, half))
    l_im = jnp.broadcast_to(lim_ref[0], (nb, half))

    def step(t, carry):
        s_re, s_im = carry
        r0 = pl.multiple_of(t * nb, nb)
        n_re = l_re * s_re - l_im * s_im + bu_scr[pl.ds(r0, nb), 0:half]
        n_im = l_re * s_im + l_im * s_re + bu_scr[pl.ds(r0, nb), half:2 * half]
        bu_scr[pl.ds(r0, nb), 0:half] = n_re
        bu_scr[pl.ds(r0, nb), half:2 * half] = n_im
        return n_re, n_im

    s_re, s_im = lax.fori_loop(0, ch, step, (st_scr[:, 0:half], st_scr[:, half:2 * half]), unroll=2)
    st_scr[:, 0:half] = s_re
    st_scr[:, half:2 * half] = s_im

    y = jnp.dot(bu_scr[...].astype(BF16), wout_ref[0], preferred_element_type=F32)
    y = jax.nn.gelu(y + d_ref[...] * u_tm)
    for s in range(nslab):
        y_scr[s] = y[:, s * LANES:(s + 1) * LANES]
    for b in range(nb):
        for s in range(nslab):
            o_ref[b, :, s * LANES:(s + 1) * LANES] = (
                y_scr[s, pl.ds(b, ch, stride=nb), :].astype(o_ref.dtype))


def s5_core(u3, lam_re, lam_im, w_in, w_out, d_skip):
    nb, length, d = u3.shape
    assert nb == SUBLANES, "the scan keeps one batch row per sublane"
    cb = w_in.shape[1]
    half = w_in.shape[2] // 2
    ch = SSM_CHUNK
    kern = functools.partial(_s5_kernel, nb=nb, ch=ch, half=half)
    return pl.pallas_call(
        kern,
        grid=(d // cb, length // ch),
        in_specs=[
            pl.BlockSpec((nb, ch, cb), lambda g, k: (0, k, g)),
            pl.BlockSpec((1, cb, 2 * half), lambda g, k: (g, 0, 0)),
            pl.BlockSpec((1, 1, half), lambda g, k: (g, 0, 0)),
            pl.BlockSpec((1, 1, half), lambda g, k: (g, 0, 0)),
            pl.BlockSpec((1, 2 * half, cb), lambda g, k: (g, 0, 0)),
            pl.BlockSpec((1, cb), lambda g, k: (0, g)),
        ],
        out_specs=pl.BlockSpec((nb, ch, cb), lambda g, k: (0, k, g)),
        out_shape=jax.ShapeDtypeStruct((nb, length, d), BF16),
        scratch_shapes=[
            pltpu.VMEM((cb // LANES, nb * ch, LANES), F32),
            pltpu.VMEM((nb, 2 * half), F32),
            pltpu.VMEM((nb * ch, 2 * half), F32),
            pltpu.VMEM((cb // LANES, nb * ch, LANES), F32),
        ],
        compiler_params=_params(("parallel", "arbitrary")),
        name="s5_core",
    )(u3, w_in, lam_re, lam_im, w_out, d_skip.reshape(1, d))


def _glu_kernel(g_ref, wv_ref, wg_ref, bv_ref, bg_ref, x_ref, o_ref):
    a = g_ref[...]
    val = jnp.dot(a, wv_ref[...], preferred_element_type=F32) + bv_ref[...]
    gate = jnp.dot(a, wg_ref[...], preferred_element_type=F32) + bg_ref[...]
    o_ref[...] = x_ref[...] + val * jax.nn.sigmoid(gate)


def glu_residual(g, w_glu, b_glu, x, tm=1024, tn=512):
    t, d = x.shape
    nj = d // tn
    b2 = b_glu.reshape(1, 2 * d)
    return pl.pallas_call(
        _glu_kernel,
        grid=(t // tm, nj),
        in_specs=[
            pl.BlockSpec((tm, d), lambda i, j: (i, 0)),
            pl.BlockSpec((d, tn), lambda i, j: (0, j)),
            pl.BlockSpec((d, tn), lambda i, j: (0, j + nj)),
            pl.BlockSpec((1, tn), lambda i, j: (0, j)),
            pl.BlockSpec((1, tn), lambda i, j: (0, j + nj)),
            pl.BlockSpec((tm, tn), lambda i, j: (i, j)),
        ],
        out_specs=pl.BlockSpec((tm, tn), lambda i, j: (i, j)),
        out_shape=jax.ShapeDtypeStruct((t, d), F32),
        compiler_params=_params(("parallel", "arbitrary")),
        name="glu_residual",
    )(g, w_glu, w_glu, b2, b2, x)


def _token_rows(d):
    return d // (2 * LANES)


def _store_token_major(ref, row0, x):
    n, d = x.shape
    nrow = _token_rows(d)
    words = pltpu.pack_elementwise([x[:, :d // 2], x[:, d // 2:]], packed_dtype=BF16)
    for c in range(nrow):
        ref[pl.ds(row0 + c, n, stride=nrow), :] = words[:, c * LANES:(c + 1) * LANES]


def _load_token_major(ref, row0, n, d):
    nrow = _token_rows(d)
    words = [ref[pl.ds(row0 + c, n, stride=nrow), :] for c in range(nrow)]
    halves = [[pltpu.unpack_elementwise(w, index=i, packed_dtype=BF16, unpacked_dtype=F32) for w in words]
              for i in (0, 1)]
    return jnp.concatenate(halves[0] + halves[1], axis=1)


def _router_kernel(h_ref, g_ref, wr_ref, br_ref, u_ref, meta_ref, cnt_ref, base_scr, *, tm):
    i = pl.program_id(0)

    @pl.when(i == 0)
    def _():
        base_scr[...] = jnp.zeros_like(base_scr)

    x = h_ref[...]
    u = x * _rms_scale(x) * g_ref[...]
    _store_token_major(u_ref, 0, u)
    logits = jnp.dot(u, wr_ref[...], preferred_element_type=F32,
                     precision=lax.Precision.HIGHEST) + br_ref[...]
    lane = lax.broadcasted_iota(I32, logits.shape, 1)
    big = jnp.int32(1 << 20)
    is_grp = (lane >= N_EXPERTS) & (lane < N_EXPERTS + N_GROUPS)
    gl = jnp.where(is_grp, logits, -jnp.inf)
    gmax = jnp.max(gl, axis=-1, keepdims=True)
    gidx = jnp.min(jnp.where(gl == gmax, lane, big), axis=-1, keepdims=True) - N_EXPERTS
    gsum = jnp.sum(jnp.where(is_grp, jnp.exp(logits - gmax), 0.0), axis=-1, keepdims=True)
    grp_w = 1.0 / gsum
    in_grp = (lane < N_EXPERTS) & (jnp.right_shift(lane, EXPERTS_PER_GROUP.bit_length() - 1) == gidx)
    sel = jnp.where(in_grp, logits, -jnp.inf)
    v1 = jnp.max(sel, axis=-1, keepdims=True)
    i1 = jnp.min(jnp.where(sel == v1, lane, big), axis=-1, keepdims=True)
    sel2 = jnp.where(lane == i1, -jnp.inf, sel)
    v2 = jnp.max(sel2, axis=-1, keepdims=True)
    i2 = jnp.min(jnp.where(sel2 == v2, lane, big), axis=-1, keepdims=True)
    e21 = jnp.exp(v2 - v1)
    w1 = grp_w / (1.0 + e21)
    w2 = grp_w * e21 / (1.0 + e21)
    oh1 = lane == i1
    oh2 = lane == i2
    oh = (oh1 | oh2).astype(F32)
    rr = lax.broadcasted_iota(I32, (tm, tm), 0)
    cc = lax.broadcasted_iota(I32, (tm, tm), 1)
    tri = (cc < rr).astype(BF16)
    before = jnp.dot(tri, oh.astype(BF16), preferred_element_type=F32) + base_scr[...]
    r1 = jnp.sum(jnp.where(oh1, before, 0.0), axis=-1, keepdims=True)
    r2 = jnp.sum(jnp.where(oh2, before, 0.0), axis=-1, keepdims=True)
    base_scr[...] = base_scr[...] + jnp.sum(oh, axis=0, keepdims=True)
    cnt_ref[...] = base_scr[...]
    meta = jnp.where(lane == 0, i1.astype(F32), 0.0)
    meta = jnp.where(lane == 1, i2.astype(F32), meta)
    meta = jnp.where(lane == 2, w1, meta)
    meta = jnp.where(lane == 3, w2, meta)
    meta = jnp.where(lane == 4, r1, meta)
    meta = jnp.where(lane == 5, r2, meta)
    meta_ref[...] = meta


def moe_router(h, norm_g, r_grp_w, r_grp_b, r_exp_w, r_exp_b):
    t, d = h.shape
    tm = ROUTER_TILE
    pad = LANES - N_EXPERTS - N_GROUPS
    wr = jnp.concatenate([r_exp_w, r_grp_w, jnp.zeros((d, pad), F32)], axis=1)
    br = jnp.concatenate([r_exp_b, r_grp_b, jnp.zeros((pad,), F32)]).reshape(1, LANES)
    return pl.pallas_call(
        functools.partial(_router_kernel, tm=tm),
        grid=(t // tm,),
        in_specs=[
            pl.BlockSpec((tm, d), lambda i: (i, 0)),
            pl.BlockSpec((1, d), lambda i: (0, 0)),
            pl.BlockSpec((d, LANES), lambda i: (0, 0)),
            pl.BlockSpec((1, LANES), lambda i: (0, 0)),
        ],
        out_specs=[
            pl.BlockSpec((tm * _token_rows(d), LANES), lambda i: (i, 0)),
            pl.BlockSpec((tm, LANES), lambda i: (i, 0)),
            pl.BlockSpec((1, LANES), lambda i: (0, 0)),
        ],
        out_shape=[
            jax.ShapeDtypeStruct((t * _token_rows(d), LANES), PACKED),
            jax.ShapeDtypeStruct((t, LANES), F32),
            jax.ShapeDtypeStruct((1, LANES), F32),
        ],
        scratch_shapes=[pltpu.VMEM((1, LANES), F32)],
        compiler_params=_params(("arbitrary",)),
        name="moe_router",
    )(h, norm_g.reshape(1, d), wr, br)


def moe_plan(meta, counts, t):
    tile = MOE_TILE
    n_tiles = (2 * t) // tile + N_EXPERTS
    cnt = counts[0, :N_EXPERTS].astype(I32)
    padded = ((cnt + tile - 1) // tile) * tile
    ends = jnp.cumsum(padded)
    offs = ends - padded
    e = meta[:, 0:2].astype(I32)
    rank = meta[:, 4:6].astype(I32)
    slot = offs[e] + rank
    row = jnp.arange(t, dtype=I32)[:, None] + jnp.array([0, t], I32)[None, :]
    row_of_slot = jnp.full((n_tiles * tile,), -1, I32).at[slot.reshape(-1)].set(row.reshape(-1))
    row_of_slot = row_of_slot.reshape(n_tiles, tile)
    valid = row_of_slot >= 0
    src = jnp.where(valid, jnp.where(row_of_slot >= t, row_of_slot - t, row_of_slot), 0)
    spare = 2 * t + (jnp.arange(n_tiles, dtype=I32)[:, None] % 2) * tile + jnp.arange(tile, dtype=I32)[None, :]
    dst = jnp.where(valid, row_of_slot, spare)
    slot_io = jnp.concatenate([src, dst], axis=1).reshape(n_tiles, 1, 2 * tile)
    starts = jnp.arange(n_tiles, dtype=I32) * tile
    tile_expert = jnp.minimum(
        jnp.sum((starts[:, None] >= ends[None, :]).astype(I32), axis=1), N_EXPERTS - 1)
    n_used = (ends[-1] // tile).astype(I32).reshape(1)
    return slot_io, tile_expert, n_used


def _moe_expert_kernel(te_ref, nu_ref, cur_ref, nxt_ref, u_hbm, wg_ref, wu_ref, wd_ref, o_hbm,
                       idx0, idx1, rows, yrows, wg_s, wu_s, wd_s, sem_i, sem_g, sem_s, *, tile, d, t):
    i = pl.program_id(0)
    n_used = nu_ref[0]
    nchunk = _token_rows(d)
    buf_rows = tile * nchunk
    cur = i % 2
    nxt = 1 - cur
    idx_smem = (idx0, idx1)

    def by_parity(par, fn):
        for b in (0, 1):
            pl.when(par == b)(functools.partial(fn, b))

    def idx_copy(src_ref, b):
        return pltpu.make_async_copy(src_ref.at[0, 0], idx_smem[b], sem_i.at[b])

    def gather_copy(tok, r, b):
        src = u_hbm.at[pl.ds(pl.multiple_of(tok * nchunk, nchunk), nchunk)]
        dst = rows.at[pl.ds(pl.multiple_of(b * buf_rows + r * nchunk, nchunk), nchunk)]
        return pltpu.make_async_copy(src, dst, sem_g.at[b])

    def scatter_copy(row, r, b):
        src = yrows.at[pl.ds(pl.multiple_of(b * buf_rows + r * nchunk, nchunk), nchunk)]
        dst = o_hbm.at[pl.ds(pl.multiple_of(row * nchunk, nchunk), nchunk)]
        return pltpu.make_async_copy(src, dst, sem_s.at[b])

    def for_rows(fn):
        def body(r, c):
            fn(r)
            return c
        lax.fori_loop(0, tile, body, 0, unroll=8)

    def start_gather(b):
        for_rows(lambda r: gather_copy(idx_smem[b][r], r, b).start())

    def wait_gather(b):
        pltpu.make_async_copy(u_hbm.at[pl.ds(0, buf_rows)], rows.at[pl.ds(b * buf_rows, buf_rows)],
                              sem_g.at[b]).wait()

    def start_scatter(b):
        for_rows(lambda r: scatter_copy(idx_smem[b][tile + r], r, b).start())

    def wait_scatter(b):
        pltpu.make_async_copy(yrows.at[pl.ds(b * buf_rows, buf_rows)], o_hbm.at[pl.ds(0, buf_rows)],
                              sem_s.at[b]).wait()

    @pl.when(i < n_used)
    def _():
        @pl.when(i == 0)
        def _():
            first = idx_copy(cur_ref, 0)
            first.start()
            first.wait()
            start_gather(0)
            yrows[pl.ds(0, buf_rows), :] = jnp.zeros((buf_rows, LANES), PACKED)
            fills = [pltpu.make_async_copy(yrows.at[pl.ds(0, buf_rows)],
                                           o_hbm.at[pl.ds((2 * t + b * tile) * nchunk, buf_rows)], sem_s.at[b])
                     for b in (0, 1)]
            for cp in fills:
                cp.start()
            for cp in fills:
                cp.wait()

        has_next = i + 1 < n_used

        @pl.when(has_next)
        def _():
            by_parity(nxt, lambda b: idx_copy(nxt_ref, b).start())

        by_parity(cur, wait_gather)

        prev = te_ref[jnp.maximum(i - 1, 0)]

        @pl.when((i == 0) | (te_ref[i] != prev))
        def _():
            wg_s[...] = wg_ref[0, 0].astype(BF16)
            wu_s[...] = wu_ref[0, 0].astype(BF16)
            wd_s[...] = wd_ref[0, 0].astype(BF16)

        @pl.when(has_next)
        def _():
            def prefetch(b):
                idx_copy(nxt_ref, b).wait()
                start_gather(b)
            by_parity(nxt, prefetch)

        x = _load_token_major(rows, pl.multiple_of(cur * buf_rows, buf_rows), tile, d).astype(BF16)
        gate = jnp.dot(x, wg_s[...], preferred_element_type=F32)
        up = jnp.dot(x, wu_s[...], preferred_element_type=F32)
        hid = (jax.nn.silu(gate) * up).astype(BF16)
        y = jnp.dot(hid, wd_s[...], preferred_element_type=F32)

        @pl.when(i >= 2)
        def _():
            by_parity(cur, wait_scatter)

        _store_token_major(yrows, pl.multiple_of(cur * buf_rows, buf_rows), y)
        by_parity(cur, start_scatter)

        @pl.when(i == n_used - 1)
        def _():
            @pl.when(i >= 1)
            def _():
                by_parity(nxt, wait_scatter)

            by_parity(cur, wait_scatter)


def moe_experts(u_rows, slot_io, tile_expert, n_used, w_gate, w_up, w_down, layer, t, d):
    n_tiles, _, two_tile = slot_io.shape
    tile = two_tile // 2
    hdim = w_gate.shape[-1]
    nchunk = _token_rows(d)
    last = n_tiles - 1
    grid_spec = pltpu.PrefetchScalarGridSpec(
        num_scalar_prefetch=2,
        grid=(n_tiles,),
        in_specs=[
            pl.BlockSpec((1, 1, two_tile), lambda i, te, nu: (i, 0, 0)),
            pl.BlockSpec((1, 1, two_tile), lambda i, te, nu: (jnp.minimum(i + 1, last), 0, 0)),
            pl.BlockSpec(memory_space=pl.ANY),
            pl.BlockSpec((1, 1, d, hdim), lambda i, te, nu: (layer, te[i], 0, 0)),
            pl.BlockSpec((1, 1, d, hdim), lambda i, te, nu: (layer, te[i], 0, 0)),
            pl.BlockSpec((1, 1, hdim, d), lambda i, te, nu: (layer, te[i], 0, 0)),
        ],
        out_specs=pl.BlockSpec(memory_space=pl.ANY),
        scratch_shapes=[
            pltpu.SMEM((two_tile,), I32),
            pltpu.SMEM((two_tile,), I32),
            pltpu.VMEM((2 * tile * nchunk, LANES), PACKED),
            pltpu.VMEM((2 * tile * nchunk, LANES), PACKED),
            pltpu.VMEM((d, hdim), BF16),
            pltpu.VMEM((d, hdim), BF16),
            pltpu.VMEM((hdim, d), BF16),
            pltpu.SemaphoreType.DMA((2,)),
            pltpu.SemaphoreType.DMA((2,)),
            pltpu.SemaphoreType.DMA((2,)),
        ],
    )
    return pl.pallas_call(
        functools.partial(_moe_expert_kernel, tile=tile, d=d, t=t),
        grid_spec=grid_spec,
        out_shape=jax.ShapeDtypeStruct(((2 * t + 2 * tile) * nchunk, LANES), PACKED),
        compiler_params=_params(("arbitrary",)),
        name="moe_experts",
    )(tile_expert, n_used, slot_io, slot_io, u_rows, w_gate, w_up, w_down)


def _combine_ple_kernel(y1_ref, y2_ref, h_ref, meta_ref, p_ref, g_ref, wgate_ref, wproj_ref, fg_ref,
                        o_ref, *, tm, final_norm):
    d = h_ref.shape[1]
    meta = meta_ref[...]
    y1 = _load_token_major(y1_ref, 0, tm, d)
    y2 = _load_token_major(y2_ref, 0, tm, d)
    h2 = h_ref[...] + (meta[:, 2:3] * y1 + meta[:, 3:4] * y2)
    a = (h2 * _rms_scale(h2) * g_ref[...]).astype(BF16)
    gate = jax.nn.sigmoid(jnp.dot(a, wgate_ref[...], preferred_element_type=F32))
    proj = jnp.dot(p_ref[...].astype(BF16), wproj_ref[...], preferred_element_type=F32)
    h3 = h2 + gate * proj
    if final_norm:
        h3 = h3 * _rms_scale(h3) * fg_ref[...]
    o_ref[...] = h3


def moe_combine_ple(y_rows, h, meta, p_all, layer, ple_norm_g, ple_gate_w, ple_proj_w, final_g, final_norm,
                    tm=256):
    t, d = h.shape
    pd = p_all.shape[-1]
    nchunk = _token_rows(d)
    nt = t // tm
    return pl.pallas_call(
        functools.partial(_combine_ple_kernel, tm=tm, final_norm=final_norm),
        grid=(nt,),
        in_specs=[
            pl.BlockSpec((tm * nchunk, LANES), lambda i: (i, 0)),
            pl.BlockSpec((tm * nchunk, LANES), lambda i: (nt + i, 0)),
            pl.BlockSpec((tm, d), lambda i: (i, 0)),
            pl.BlockSpec((tm, LANES), lambda i: (i, 0)),
            pl.BlockSpec((tm, pd), lambda i: (layer * nt + i, 0)),
            pl.BlockSpec((1, d), lambda i: (0, 0)),
            pl.BlockSpec((d, d), lambda i: (0, 0)),
            pl.BlockSpec((pd, d), lambda i: (0, 0)),
            pl.BlockSpec((1, d), lambda i: (0, 0)),
        ],
        out_specs=pl.BlockSpec((tm, d), lambda i: (i, 0)),
        out_shape=jax.ShapeDtypeStruct((t, d), F32),
        compiler_params=_params(("parallel",)),
        name="moe_combine_ple",
    )(y_rows, y_rows, h, meta, p_all, ple_norm_g.reshape(1, d), ple_gate_w, ple_proj_w, final_g.reshape(1, d))


def moe_ple_layer(h, p_all, layer, ffn_norm_g, r_grp_w, r_grp_b, r_exp_w, r_exp_b, w_gate, w_up, w_down,
                  ple_norm_g, ple_gate_w, ple_proj_w, final_g, final_norm):
    t, d = h.shape
    u_rows, meta, counts = moe_router(h, ffn_norm_g, r_grp_w, r_grp_b, r_exp_w, r_exp_b)
    slot_io, tile_expert, n_used = moe_plan(meta, counts, t)
    y_rows = moe_experts(u_rows, slot_io, tile_expert, n_used, w_gate, w_up, w_down, layer, t, d)
    return moe_combine_ple(y_rows, h, meta, p_all, layer, ple_norm_g, ple_gate_w, ple_proj_w, final_g,
                           final_norm)


def _qkv_kernel(x_ref, gq_ref, gkv_ref, w_ref, cos_ref, sin_ref, o_ref, act_scr, *, nq_tiles):
    j = pl.program_id(1)

    @pl.when(j == 0)
    def _():
        x = x_ref[...]
        xn = x * _rms_scale(x)
        act_scr[0] = (xn * gq_ref[...]).astype(BF16)
        act_scr[1] = (xn * gkv_ref[...]).astype(BF16)

    a = act_scr[(j >= nq_tiles).astype(I32)]
    acc = jnp.dot(a, w_ref[...], preferred_element_type=F32)
    cos = cos_ref[0]
    sin = sin_ref[0]
    for hd in range(acc.shape[1] // HEAD_DIM):
        cols = slice(hd * HEAD_DIM, (hd + 1) * HEAD_DIM)
        seg = acc[:, cols]
        o_ref[:, cols] = (seg * cos + pltpu.roll(seg, HEAD_DIM // 2, 1) * sin).astype(o_ref.dtype)


def qkv_project(h, attn_g, kv_g, w_all, cos_t, sin_t, length, tm=512, tn=1024):
    t, d = h.shape
    n_all = w_all.shape[1]
    kvw = N_KV_HEADS * HEAD_DIM
    assert tn == kvw
    nq_tiles = (n_all - 2 * kvw) // tn
    nl = length // tm

    def table_spec():
        return pl.BlockSpec((1, tm, HEAD_DIM), lambda i, j: (jnp.clip(j - (nq_tiles - 1), 0, 2), i % nl, 0))

    return pl.pallas_call(
        functools.partial(_qkv_kernel, nq_tiles=nq_tiles),
        grid=(t // tm, n_all // tn),
        in_specs=[
            pl.BlockSpec((tm, d), lambda i, j: (i, 0)),
            pl.BlockSpec((1, d), lambda i, j: (0, 0)),
            pl.BlockSpec((1, d), lambda i, j: (0, 0)),
            pl.BlockSpec((d, tn), lambda i, j: (0, j)),
            table_spec(),
            table_spec(),
        ],
        out_specs=pl.BlockSpec((tm, tn), lambda i, j: (i, j)),
        out_shape=jax.ShapeDtypeStruct((t, n_all), BF16),
        scratch_shapes=[pltpu.VMEM((2, tm, d), BF16)],
        compiler_params=_params(("parallel", "arbitrary")),
        name="qkv_project",
    )(h, attn_g.reshape(1, d), kv_g.reshape(1, d), w_all, cos_t, sin_t)


def _attn_kernel(q0_ref, q1_ref, q2_ref, k_ref, v_ref, o_ref, q_scr, k_scr, v_scr, o_scr, lse_scr, bias_scr,
                 stage_scr, ostage_scr, *, length, patterns, unroll):
    blk = ATTN_BLOCK
    npat = len(patterns)
    q_refs = (q0_ref, q1_ref, q2_ref)
    for g in range(npat):
        for rep in range(KV_REP):
            q_scr[g * KV_REP + rep] = q_refs[g][:, rep * HEAD_DIM:(rep + 1) * HEAD_DIM].astype(F32)
    k_scr[...] = k_ref[...].astype(F32)
    v_scr[...] = v_ref[...].astype(F32)

    n_steps = patterns[0][0] // patterns[0][1]
    assert all(w // dl == n_steps for w, dl in patterns)
    qi = lax.broadcasted_iota(I32, (blk, 2 * blk), 0)
    kj = lax.broadcasted_iota(I32, (blk, 2 * blk), 1)
    for first in (0, 1):
        dist = qi - kj + first * blk
        bias_scr[first] = jnp.where((dist >= 0) & (dist <= n_steps), 0.0, NEG_INF)

    def attend(q, kk, vv, bias):
        s = lax.dot_general(q, kk, (((1,), (1,)), ((), ())), preferred_element_type=F32) + bias
        m = jnp.max(s, axis=-1, keepdims=True)
        e = jnp.exp(s - m)
        den = jnp.sum(e, axis=-1, keepdims=True)
        pv = jnp.dot(e.astype(BF16), vv, preferred_element_type=F32)
        return pv / den, jnp.broadcast_to(m + jnp.log(den), (blk, HEAD_DIM))

    for g, (window, dil) in enumerate(patterns):
        m_len = length // dil
        nblk = m_len // blk
        win = min(2 * blk, m_len)
        staged = dil > MAX_CHEAP_SUBLANE_STRIDE

        if staged:
            win = 2 * blk
            srcs = [q_scr.at[g * KV_REP + rep] for rep in range(KV_REP)] + [k_scr, v_scr]
            for a, src in enumerate(srcs):
                for res in range(dil):
                    stage_scr[a, res * m_len:(res + 1) * m_len, :] = (
                        src[pl.ds(res, m_len, stride=dil), :].astype(BF16))
                stage_scr[a, length:length + blk, :] = jnp.zeros((blk, HEAD_DIM), BF16)

        def tile(idx, c, g=g, dil=dil, nblk=nblk, win=win, m_len=m_len, staged=staged):
            r = idx // nblk
            n = idx % nblk
            ks = jnp.maximum(n - 1, 0) * blk
            bias = bias_scr[jnp.minimum(n, 1), :, 0:win]
            if staged:
                q_rows = pl.ds(pl.multiple_of(r * m_len + n * blk, blk), blk)
                k_rows = pl.ds(pl.multiple_of(r * m_len + ks, blk), win)
                kk = stage_scr[KV_REP, k_rows, :]
                vv = stage_scr[KV_REP + 1, k_rows, :]
            else:
                q_rows = pl.ds(n * (blk * dil) + r, blk, stride=dil)
                k_rows = pl.ds(ks * dil + r, win, stride=dil)
                kk = k_scr[k_rows, :].astype(BF16)
                vv = v_scr[k_rows, :].astype(BF16)
            for rep in range(KV_REP):
                slot = g * KV_REP + rep
                if staged:
                    o, lse = attend(stage_scr[rep, q_rows, :], kk, vv, bias)
                    ostage_scr[rep, q_rows, :] = o
                    ostage_scr[KV_REP + rep, q_rows, :] = lse
                else:
                    o, lse = attend(q_scr[slot, q_rows, :].astype(BF16), kk, vv, bias)
                    o_scr[slot, q_rows, :] = o
                    lse_scr[slot, q_rows, :] = lse
            return c

        lax.fori_loop(0, dil * nblk, tile, 0, unroll=unroll)

        if staged:
            for rep in range(KV_REP):
                slot = g * KV_REP + rep
                for res in range(dil):
                    rows = pl.ds(res, m_len, stride=dil)
                    o_scr[slot, rows, :] = ostage_scr[rep, res * m_len:(res + 1) * m_len, :]
                    lse_scr[slot, rows, :] = ostage_scr[KV_REP + rep, res * m_len:(res + 1) * m_len, :]

    for rep in range(KV_REP):
        lses = [lse_scr[g * KV_REP + rep] for g in range(npat)]
        top = functools.reduce(jnp.maximum, lses)
        ws = [jnp.exp(l - top) for l in lses]
        num = sum(w * o_scr[g * KV_REP + rep] for g, w in enumerate(ws))
        o_ref[:, rep * HEAD_DIM:(rep + 1) * HEAD_DIM] = (num / sum(ws)).astype(o_ref.dtype)


def dilated_attention(qkv, bsz, length):
    t = qkv.shape[0]
    qw = KV_REP * HEAD_DIM
    npat = len(DILATED_PATTERNS)
    k_base = npat * N_Q_HEADS
    v_base = k_base + N_KV_HEADS
    for _, dil in DILATED_PATTERNS:
        assert (length // dil) % ATTN_BLOCK == 0

    def q_spec(g):
        return pl.BlockSpec((length, qw), lambda b, h: (b, g * N_KV_HEADS + h))

    return pl.pallas_call(
        functools.partial(_attn_kernel, length=length, patterns=DILATED_PATTERNS, unroll=16),
        grid=(bsz, N_KV_HEADS),
        in_specs=[
            q_spec(0), q_spec(1), q_spec(2),
            pl.BlockSpec((length, HEAD_DIM), lambda b, h: (b, k_base + h)),
            pl.BlockSpec((length, HEAD_DIM), lambda b, h: (b, v_base + h)),
        ],
        out_specs=pl.BlockSpec((length, qw), lambda b, h: (b, h)),
        out_shape=jax.ShapeDtypeStruct((t, N_Q_HEADS * HEAD_DIM), BF16),
        scratch_shapes=[
            pltpu.VMEM((npat * KV_REP, length, HEAD_DIM), F32),
            pltpu.VMEM((length, HEAD_DIM), F32),
            pltpu.VMEM((length, HEAD_DIM), F32),
            pltpu.VMEM((npat * KV_REP, length, HEAD_DIM), F32),
            pltpu.VMEM((npat * KV_REP, length, HEAD_DIM), F32),
            pltpu.VMEM((2, ATTN_BLOCK, 2 * ATTN_BLOCK), F32),
            pltpu.VMEM((KV_REP + 2, length + ATTN_BLOCK, HEAD_DIM), BF16),
            pltpu.VMEM((2 * KV_REP, length, HEAD_DIM), F32),
        ],
        compiler_params=_params(("parallel", "parallel")),
        name="dilated_attention",
    )(qkv, qkv, qkv, qkv, qkv)


def _proj_residual_kernel(a_ref, w_ref, x_ref, o_ref):
    o_ref[...] = x_ref[...] + jnp.dot(a_ref[...], w_ref[...], preferred_element_type=F32)


def proj_residual(a, w, x, tm=1024, tn=512):
    t, d = x.shape
    kdim = a.shape[1]
    return pl.pallas_call(
        _proj_residual_kernel,
        grid=(t // tm, d // tn),
        in_specs=[
            pl.BlockSpec((tm, kdim), lambda i, j: (i, 0)),
            pl.BlockSpec((kdim, tn), lambda i, j: (0, j)),
            pl.BlockSpec((tm, tn), lambda i, j: (i, j)),
        ],
        out_specs=pl.BlockSpec((tm, tn), lambda i, j: (i, j)),
        out_shape=jax.ShapeDtypeStruct((t, d), F32),
        compiler_params=_params(("parallel", "arbitrary")),
        name="proj_residual",
    )(a, w, x)


def _rope_tables(length):
    inv = 1.0 / (ROPE_THETA ** (jnp.arange(0, HEAD_DIM, 2, dtype=F32) / HEAD_DIM))
    ang = jnp.arange(length, dtype=F32)[:, None] * inv[None, :]
    cos, sin = jnp.cos(ang), jnp.sin(ang)
    cos_f = jnp.concatenate([cos, cos], axis=-1)
    sin_f = jnp.concatenate([-sin, sin], axis=-1)
    scale = HEAD_DIM ** -0.5
    cos_t = jnp.stack([cos_f * scale, cos_f, jnp.ones_like(cos_f)])
    sin_t = jnp.stack([sin_f * scale, sin_f, jnp.zeros_like(sin_f)])
    return cos_t, sin_t


def kernel(x, p, s5_norm_g, s5_a_re, s5_a_im, s5_log_dt, s5_b_re, s5_b_im, s5_c_re, s5_c_im, s5_d, s5_w_glu, s5_b_glu, kv_norm_g, w_k, w_v, attn_norm_g, w_q, w_o, ffn_norm_g, router_grp_w, router_grp_b, router_exp_w, router_exp_b, w_gate, w_up, w_down, ple_norm_g, ple_gate_w, ple_proj_w, final_norm_g):
    bsz, length, d = x.shape
    depth = p.shape[0]
    n_s5 = s5_norm_g.shape[0]
    t = bsz * length
    h = x.reshape(t, d)
    cos_t, sin_t = _rope_tables(length)
    for i in range(depth):
        if i < n_s5:
            lam_re, lam_im, w_in, w_out = s5_prepare(
                s5_a_re[i], s5_a_im[i], s5_log_dt[i], s5_b_re[i], s5_b_im[i], s5_c_re[i], s5_c_im[i])
            u = rmsnorm(h, s5_norm_g[i])
            g = s5_core(u.reshape(bsz, length, d), lam_re, lam_im, w_in, w_out, s5_d[i])
            h = glu_residual(g.reshape(t, d), s5_w_glu[i].astype(BF16), s5_b_glu[i], h)
        else:
            j = i - n_s5
            assert j == 0, "shared K/V reuse across several attention layers is not implemented"
            w_all = jnp.concatenate([w_q[j], w_k, w_v], axis=1).astype(BF16)
            qkv = qkv_project(h, attn_norm_g[j], kv_norm_g, w_all, cos_t, sin_t, length)
            attn = dilated_attention(qkv, bsz, length)
            h = proj_residual(attn, w_o[j].astype(BF16), h)
        h = moe_ple_layer(
            h, p.reshape(depth * t, -1), i, ffn_norm_g[i], router_grp_w[i], router_grp_b[i], router_exp_w[i],
            router_exp_b[i], w_gate, w_up, w_down, ple_norm_g[i], ple_gate_w[i].astype(BF16),
            ple_proj_w[i].astype(BF16), final_norm_g, final_norm=(i == depth - 1))
    return h.reshape(bsz, length, d)
```

```python
import functools

import jax
import jax.numpy as jnp
from jax import lax
from jax.experimental import pallas as pl
from jax.experimental.pallas import tpu as pltpu

F32 = jnp.float32
BF16 = jnp.bfloat16
I32 = jnp.int32
PACKED = jnp.uint32

NORM_EPS = 1e-6
LANES = 128
SUBLANES = 8
MXU_DIM = 256
VMEM_LIMIT_BYTES = 56 * 1024 * 1024

SSM_GROUP = 16
SSM_STATE = 64
SSM_GROUP_BLOCK = MXU_DIM // SSM_GROUP
SSM_CHUNK = 256
HEAD_DIM = 128
N_Q_HEADS = 16
N_KV_HEADS = 8
KV_REP = N_Q_HEADS // N_KV_HEADS
DILATED_PATTERNS = ((128, 1), (512, 4), (2048, 16))
ATTN_BLOCK = 128
MAX_CHEAP_SUBLANE_STRIDE = 4
ROPE_THETA = 10000.0
NEG_INF = -1e30
N_GROUPS = 4
EXPERTS_PER_GROUP = 8
N_EXPERTS = N_GROUPS * EXPERTS_PER_GROUP
MOE_TILE = 256
ROUTER_TILE = 512


def _params(sem):
    return pltpu.CompilerParams(dimension_semantics=sem, vmem_limit_bytes=VMEM_LIMIT_BYTES)


def _rms_scale(x):
    return lax.rsqrt(jnp.mean(x * x, axis=-1, keepdims=True) + NORM_EPS)


def _rmsnorm_kernel(x_ref, g_ref, o_ref):
    x = x_ref[...]
    o_ref[...] = (x * _rms_scale(x) * g_ref[...]).astype(o_ref.dtype)


def rmsnorm(x, g, out_dtype=F32, tm=512):
    t, d = x.shape
    return pl.pallas_call(
        _rmsnorm_kernel,
        grid=(t // tm,),
        in_specs=[pl.BlockSpec((tm, d), lambda i: (i, 0)),
                  pl.BlockSpec((1, d), lambda i: (0, 0))],
        out_specs=pl.BlockSpec((tm, d), lambda i: (i, 0)),
        out_shape=jax.ShapeDtypeStruct((t, d), out_dtype),
        compiler_params=_params(("parallel",)),
        name="rmsnorm",
    )(x, g.reshape(1, d))


def _s5_discretize_kernel(are_ref, aim_ref, ldt_ref, bre_ref, bim_ref,
                          lre_ref, lim_ref, bbre_ref, bbim_ref):
    a_re, a_im = are_ref[...], aim_ref[...]
    dt = jnp.exp(ldt_ref[...])
    mag = jnp.exp(dt * a_re)
    l_re = mag * jnp.cos(dt * a_im)
    l_im = mag * jnp.sin(dt * a_im)
    n_re, n_im = l_re - 1.0, l_im
    den = a_re * a_re + a_im * a_im
    f_re = (n_re * a_re + n_im * a_im) / den
    f_im = (n_im * a_re - n_re * a_im) / den
    b_re, b_im = bre_ref[...], bim_ref[...]
    lre_ref[...] = l_re
    lim_ref[...] = l_im
    bbre_ref[...] = f_re * b_re - f_im * b_im
    bbim_ref[...] = f_re * b_im + f_im * b_re


def _block_diag(m):
    nb, gb, a, b = m.shape
    eye = jnp.eye(gb, dtype=m.dtype)
    return jnp.einsum("ngab,gh->ngahb", m, eye).reshape(nb, gb * a, gb * b)


def s5_prepare(a_re, a_im, log_dt, b_re, b_im, c_re, c_im):
    g, p = a_re.shape
    gc = b_re.shape[-1]
    rep = lambda v: jnp.repeat(v, gc, axis=-1)
    shp = jax.ShapeDtypeStruct((g, p * gc), F32)
    l_re, l_im, bb_re, bb_im = pl.pallas_call(
        _s5_discretize_kernel, out_shape=(shp, shp, shp, shp), name="s5_discretize",
    )(rep(a_re), rep(a_im), rep(jnp.broadcast_to(log_dt[:, None], (g, p))),
      b_re.reshape(g, p * gc), b_im.reshape(g, p * gc))
    nb = g // SSM_GROUP_BLOCK
    lam_re = l_re[:, ::gc].reshape(nb, 1, SSM_GROUP_BLOCK * p)
    lam_im = l_im[:, ::gc].reshape(nb, 1, SSM_GROUP_BLOCK * p)

    def in_mat(bb):
        m = bb.reshape(g, p, gc).transpose(0, 2, 1).reshape(nb, SSM_GROUP_BLOCK, gc, p)
        return _block_diag(m)

    def out_mat(c):
        m = c.transpose(0, 2, 1).reshape(nb, SSM_GROUP_BLOCK, p, gc)
        return _block_diag(m)

    w_in = jnp.concatenate([in_mat(bb_re), in_mat(bb_im)], axis=-1).astype(BF16)
    w_out = jnp.concatenate([out_mat(c_re), out_mat(-c_im)], axis=1).astype(BF16)
    return lam_re, lam_im, w_in, w_out


def _s5_kernel(u_ref, win_ref, lre_ref, lim_ref, wout_ref, d_ref, o_ref,
               lhs_scr, st_scr, bu_scr, y_scr, *, nb, ch, half):
    k = pl.program_id(1)
    rows = nb * ch
    nslab = lhs_scr.shape[0]

    @pl.when(k == 0)
    def _():
        st_scr[...] = jnp.zeros_like(st_scr)

    for b in range(nb):
        for s in range(nslab):
            lhs_scr[s, pl.ds(b, ch, stride=nb), :] = u_ref[b, :, s * LANES:(s + 1) * LANES]
    u_tm = jnp.concatenate([lhs_scr[s] for s in range(nslab)], axis=1)
    bu_scr[...] = jnp.dot(u_tm.astype(BF16), win_ref[0], preferred_element_type=F32)

    l_re = jnp.broadcast_to(lre_ref[0], (nb, half))
    l_im = jnp.broadcast_to(lim_ref[0], (nb, half))

    def step(t, carry):
        s_re, s_im = carry
        r0 = pl.multiple_of(t * nb, nb)
        n_re = l_re * s_re - l_im * s_im + bu_scr[pl.ds(r0, nb), 0:half]
        n_im = l_re * s_im + l_im * s_re + bu_scr[pl.ds(r0, nb), half:2 * half]
        bu_scr[pl.ds(r0, nb), 0:half] = n_re
        bu_scr[pl.ds(r0, nb), half:2 * half] = n_im
        return n_re, n_im

    s_re, s_im = lax.fori_loop(0, ch, step, (st_scr[:, 0:half], st_scr[:, half:2 * half]), unroll=2)
    st_scr[:, 0:half] = s_re
    st_scr[:, half:2 * half] = s_im

    y = jnp.dot(bu_scr[...].astype(BF16), wout_ref[0], preferred_element_type=F32)
    y = jax.nn.gelu(y + d_ref[...] * u_tm)
    for s in range(nslab):
        y_scr[s] = y[:, s * LANES:(s + 1) * LANES]
    for b in range(nb):
        for s in range(nslab):
            o_ref[b, :, s * LANES:(s + 1) * LANES] = (
                y_scr[s, pl.ds(b, ch, stride=nb), :].astype(o_ref.dtype))


def s5_core(u3, lam_re, lam_im, w_in, w_out, d_skip):
    nb, length, d = u3.shape
    assert nb == SUBLANES, "the scan keeps one batch row per sublane"
    cb = w_in.shape[1]
    half = w_in.shape[2] // 2
    ch = SSM_CHUNK
    kern = functools.partial(_s5_kernel, nb=nb, ch=ch, half=half)
    return pl.pallas_call(
        kern,
        grid=(d // cb, length // ch),
        in_specs=[
            pl.BlockSpec((nb, ch, cb), lambda g, k: (0, k, g)),
            pl.BlockSpec((1, cb, 2 * half), lambda g, k: (g, 0, 0)),
            pl.BlockSpec((1, 1, half), lambda g, k: (g, 0, 0)),
            pl.BlockSpec((1, 1, half), lambda g, k: (g, 0, 0)),
            pl.BlockSpec((1, 2 * half, cb), lambda g, k: (g, 0, 0)),
            pl.BlockSpec((1, cb), lambda g, k: (0, g)),
        ],
        out_specs=pl.BlockSpec((nb, ch, cb), lambda g, k: (0, k, g)),
        out_shape=jax.ShapeDtypeStruct((nb, length, d), BF16),
        scratch_shapes=[
            pltpu.VMEM((cb // LANES, nb * ch, LANES), F32),
            pltpu.VMEM((nb, 2 * half), F32),
            pltpu.VMEM((nb * ch, 2 * half), F32),
            pltpu.VMEM((cb // LANES, nb * ch, LANES), F32),
        ],
        compiler_params=_params(("parallel", "arbitrary")),
        name="s5_core",
    )(u3, w_in, lam_re, lam_im, w_out, d_skip.reshape(1, d))


def _glu_kernel(g_ref, wv_ref, wg_ref, bv_ref, bg_ref, x_ref, o_ref):
    a = g_ref[...]
    val = jnp.dot(a, wv_ref[...], preferred_element_type=F32) + bv_ref[...]
    gate = jnp.dot(a, wg_ref[...], preferred_element_type=F32) + bg_ref[...]
    o_ref[...] = x_ref[...] + val * jax.nn.sigmoid(gate)


def glu_residual(g, w_glu, b_glu, x, tm=1024, tn=512):
    t, d = x.shape
    nj = d // tn
    b2 = b_glu.reshape(1, 2 * d)
    return pl.pallas_call(
        _glu_kernel,
        grid=(t // tm, nj),
        in_specs=[
            pl.BlockSpec((tm, d), lambda i, j: (i, 0)),
            pl.BlockSpec((d, tn), lambda i, j: (0, j)),
            pl.BlockSpec((d, tn), lambda i, j: (0, j + nj)),
            pl.BlockSpec((1, tn), lambda i, j: (0, j)),
            pl.BlockSpec((1, tn), lambda i, j: (0, j + nj)),
            pl.BlockSpec((tm, tn), lambda i, j: (i, j)),
        ],
        out_specs=pl.BlockSpec((tm, tn), lambda i, j: (i, j)),
        out_shape=jax.ShapeDtypeStruct((t, d), F32),
        compiler_params=_params(("parallel", "arbitrary")),
        name="glu_residual",
    )(g, w_glu, w_glu, b2, b2, x)


def _token_rows(d):
    return d // (2 * LANES)


def _store_token_major(ref, row0, x):
    n, d = x.shape
    nrow = _token_rows(d)
    words = pltpu.pack_elementwise([x[:, :d // 2], x[:, d // 2:]], packed_dtype=BF16)
    for c in range(nrow):
        ref[pl.ds(row0 + c, n, stride=nrow), :] = words[:, c * LANES:(c + 1) * LANES]


def _load_token_major(ref, row0, n, d):
    nrow = _token_rows(d)
    words = [ref[pl.ds(row0 + c, n, stride=nrow), :] for c in range(nrow)]
    halves = [[pltpu.unpack_elementwise(w, index=i, packed_dtype=BF16, unpacked_dtype=F32) for w in words]
              for i in (0, 1)]
    return jnp.concatenate(halves[0] + halves[1], axis=1)


def _router_kernel(h_ref, g_ref, wr_ref, br_ref, u_ref, meta_ref, cnt_ref, base_scr, *, tm):
    i = pl.program_id(0)

    @pl.when(i == 0)
    def _():
        base_scr[...] = jnp.zeros_like(base_scr)

    x = h_ref[...]
    u = x * _rms_scale(x) * g_ref[...]
    _store_token_major(u_ref, 0, u)
    logits = jnp.dot(u, wr_ref[...], preferred_element_type=F32,
                     precision=lax.Precision.HIGHEST) + br_ref[...]
    lane = lax.broadcasted_iota(I32, logits.shape, 1)
    big = jnp.int32(1 << 20)
    is_grp = (lane >= N_EXPERTS) & (lane < N_EXPERTS + N_GROUPS)
    gl = jnp.where(is_grp, logits, -jnp.inf)
    gmax = jnp.max(gl, axis=-1, keepdims=True)
    gidx = jnp.min(jnp.where(gl == gmax, lane, big), axis=-1, keepdims=True) - N_EXPERTS
    gsum = jnp.sum(jnp.where(is_grp, jnp.exp(logits - gmax), 0.0), axis=-1, keepdims=True)
    grp_w = 1.0 / gsum
    in_grp = (lane < N_EXPERTS) & (jnp.right_shift(lane, EXPERTS_PER_GROUP.bit_length() - 1) == gidx)
    sel = jnp.where(in_grp, logits, -jnp.inf)
    v1 = jnp.max(sel, axis=-1, keepdims=True)
    i1 = jnp.min(jnp.where(sel == v1, lane, big), axis=-1, keepdims=True)
    sel2 = jnp.where(lane == i1, -jnp.inf, sel)
    v2 = jnp.max(sel2, axis=-1, keepdims=True)
    i2 = jnp.min(jnp.where(sel2 == v2, lane, big), axis=-1, keepdims=True)
    e21 = jnp.exp(v2 - v1)
    w1 = grp_w / (1.0 + e21)
    w2 = grp_w * e21 / (1.0 + e21)
    oh1 = lane == i1
    oh2 = lane == i2
    oh = (oh1 | oh2).astype(F32)
    rr = lax.broadcasted_iota(I32, (tm, tm), 0)
    cc = lax.broadcasted_iota(I32, (tm, tm), 1)
    tri = (cc < rr).astype(BF16)
    before = jnp.dot(tri, oh.astype(BF16), preferred_element_type=F32) + base_scr[...]
    r1 = jnp.sum(jnp.where(oh1, before, 0.0), axis=-1, keepdims=True)
    r2 = jnp.sum(jnp.where(oh2, before, 0.0), axis=-1, keepdims=True)
    base_scr[...] = base_scr[...] + jnp.sum(oh, axis=0, keepdims=True)
    cnt_ref[...] = base_scr[...]
    meta = jnp.where(lane == 0, i1.astype(F32), 0.0)
    meta = jnp.where(lane == 1, i2.astype(F32), meta)
    meta = jnp.where(lane == 2, w1, meta)
    meta = jnp.where(lane == 3, w2, meta)
    meta = jnp.where(lane == 4, r1, meta)
    meta = jnp.where(lane == 5, r2, meta)
    meta_ref[...] = meta


def moe_router(h, norm_g, r_grp_w, r_grp_b, r_exp_w, r_exp_b):
    t, d = h.shape
    tm = ROUTER_TILE
    pad = LANES - N_EXPERTS - N_GROUPS
    wr = jnp.concatenate([r_exp_w, r_grp_w, jnp.zeros((d, pad), F32)], axis=1)
    br = jnp.concatenate([r_exp_b, r_grp_b, jnp.zeros((pad,), F32)]).reshape(1, LANES)
    return pl.pallas_call(
        functools.partial(_router_kernel, tm=tm),
        grid=(t // tm,),
        in_specs=[
            pl.BlockSpec((tm, d), lambda i: (i, 0)),
            pl.BlockSpec((1, d), lambda i: (0, 0)),
            pl.BlockSpec((d, LANES), lambda i: (0, 0)),
            pl.BlockSpec((1, LANES), lambda i: (0, 0)),
        ],
        out_specs=[
            pl.BlockSpec((tm * _token_rows(d), LANES), lambda i: (i, 0)),
            pl.BlockSpec((tm, LANES), lambda i: (i, 0)),
            pl.BlockSpec((1, LANES), lambda i: (0, 0)),
        ],
        out_shape=[
            jax.ShapeDtypeStruct((t * _token_rows(d), LANES), PACKED),
            jax.ShapeDtypeStruct((t, LANES), F32),
            jax.ShapeDtypeStruct((1, LANES), F32),
        ],
        scratch_shapes=[pltpu.VMEM((1, LANES), F32)],
        compiler_params=_params(("arbitrary",)),
        name="moe_router",
    )(h, norm_g.reshape(1, d), wr, br)


def moe_plan(meta, counts, t):
    tile = MOE_TILE
    n_tiles = (2 * t) // tile + N_EXPERTS
    cnt = counts[0, :N_EXPERTS].astype(I32)
    padded = ((cnt + tile - 1) // tile) * tile
    ends = jnp.cumsum(padded)
    offs = ends - padded
    e = meta[:, 0:2].astype(I32)
    rank = meta[:, 4:6].astype(I32)
    slot = offs[e] + rank
    row = jnp.arange(t, dtype=I32)[:, None] + jnp.array([0, t], I32)[None, :]
    row_of_slot = jnp.full((n_tiles * tile,), -1, I32).at[slot.reshape(-1)].set(row.reshape(-1))
    row_of_slot = row_of_slot.reshape(n_tiles, tile)
    valid = row_of_slot >= 0
    src = jnp.where(valid, jnp.where(row_of_slot >= t, row_of_slot - t, row_of_slot), 0)
    spare = 2 * t + (jnp.arange(n_tiles, dtype=I32)[:, None] % 2) * tile + jnp.arange(tile, dtype=I32)[None, :]
    dst = jnp.where(valid, row_of_slot, spare)
    slot_io = jnp.concatenate([src, dst], axis=1).reshape(n_tiles, 1, 2 * tile)
    starts = jnp.arange(n_tiles, dtype=I32) * tile
    tile_expert = jnp.minimum(
        jnp.sum((starts[:, None] >= ends[None, :]).astype(I32), axis=1), N_EXPERTS - 1)
    n_used = (ends[-1] // tile).astype(I32).reshape(1)
    return slot_io, tile_expert, n_used


def _moe_expert_kernel(te_ref, nu_ref, cur_ref, nxt_ref, u_hbm, wg_ref, wu_ref, wd_ref, o_hbm,
                       idx0, idx1, rows, yrows, wg_s, wu_s, wd_s, sem_i, sem_g, sem_s, *, tile, d, t):
    i = pl.program_id(0)
    n_used = nu_ref[0]
    nchunk = _token_rows(d)
    buf_rows = tile * nchunk
    cur = i % 2
    nxt = 1 - cur
    idx_smem = (idx0, idx1)

    def by_parity(par, fn):
        for b in (0, 1):
            pl.when(par == b)(functools.partial(fn, b))

    def idx_copy(src_ref, b):
        return pltpu.make_async_copy(src_ref.at[0, 0], idx_smem[b], sem_i.at[b])

    def gather_copy(tok, r, b):
        src = u_hbm.at[pl.ds(pl.multiple_of(tok * nchunk, nchunk), nchunk)]
        dst = rows.at[pl.ds(pl.multiple_of(b * buf_rows + r * nchunk, nchunk), nchunk)]
        return pltpu.make_async_copy(src, dst, sem_g.at[b])

    def scatter_copy(row, r, b):
        src = yrows.at[pl.ds(pl.multiple_of(b * buf_rows + r * nchunk, nchunk), nchunk)]
        dst = o_hbm.at[pl.ds(pl.multiple_of(row * nchunk, nchunk), nchunk)]
        return pltpu.make_async_copy(src, dst, sem_s.at[b])

    def for_rows(fn):
        def body(r, c):
            fn(r)
            return c
        lax.fori_loop(0, tile, body, 0, unroll=8)

    def start_gather(b):
        for_rows(lambda r: gather_copy(idx_smem[b][r], r, b).start())

    def wait_gather(b):
        pltpu.make_async_copy(u_hbm.at[pl.ds(0, buf_rows)], rows.at[pl.ds(b * buf_rows, buf_rows)],
                              sem_g.at[b]).wait()

    def start_scatter(b):
        for_rows(lambda r: scatter_copy(idx_smem[b][tile + r], r, b).start())

    def wait_scatter(b):
        pltpu.make_async_copy(yrows.at[pl.ds(b * buf_rows, buf_rows)], o_hbm.at[pl.ds(0, buf_rows)],
                              sem_s.at[b]).wait()

    @pl.when(i < n_used)
    def _():
        @pl.when(i == 0)
        def _():
            first = idx_copy(cur_ref, 0)
            first.start()
            first.wait()
            start_gather(0)
            yrows[pl.ds(0, buf_rows), :] = jnp.zeros((buf_rows, LANES), PACKED)
            fills = [pltpu.make_async_copy(yrows.at[pl.ds(0, buf_rows)],
                                           o_hbm.at[pl.ds((2 * t + b * tile) * nchunk, buf_rows)], sem_s.at[b])
                     for b in (0, 1)]
            for cp in fills:
                cp.start()
            for cp in fills:
                cp.wait()

        has_next = i + 1 < n_used

        @pl.when(has_next)
        def _():
            by_parity(nxt, lambda b: idx_copy(nxt_ref, b).start())

        by_parity(cur, wait_gather)

        prev = te_ref[jnp.maximum(i - 1, 0)]

        @pl.when((i == 0) | (te_ref[i] != prev))
        def _():
            wg_s[...] = wg_ref[0, 0].astype(BF16)
            wu_s[...] = wu_ref[0, 0].astype(BF16)
            wd_s[...] = wd_ref[0, 0].astype(BF16)

        @pl.when(has_next)
        def _():
            def prefetch(b):
                idx_copy(nxt_ref, b).wait()
                start_gather(b)
            by_parity(nxt, prefetch)

        x = _load_token_major(rows, pl.multiple_of(cur * buf_rows, buf_rows), tile, d).astype(BF16)
        gate = jnp.dot(x, wg_s[...], preferred_element_type=F32)
        up = jnp.dot(x, wu_s[...], preferred_element_type=F32)
        hid = (jax.nn.silu(gate) * up).astype(BF16)
        y = jnp.dot(hid, wd_s[...], preferred_element_type=F32)

        @pl.when(i >= 2)
        def _():
            by_parity(cur, wait_scatter)

        _store_token_major(yrows, pl.multiple_of(cur * buf_rows, buf_rows), y)
        by_parity(cur, start_scatter)

        @pl.when(i == n_used - 1)
        def _():
            @pl.when(i >= 1)
            def _():
                by_parity(nxt, wait_scatter)

            by_parity(cur, wait_scatter)


def moe_experts(u_rows, slot_io, tile_expert, n_used, w_gate, w_up, w_down, layer, t, d):
    n_tiles, _, two_tile = slot_io.shape
    tile = two_tile // 2
    hdim = w_gate.shape[-1]
    nchunk = _token_rows(d)
    last = n_tiles - 1
    grid_spec = pltpu.PrefetchScalarGridSpec(
        num_scalar_prefetch=2,
        grid=(n_tiles,),
        in_specs=[
            pl.BlockSpec((1, 1, two_tile), lambda i, te, nu: (i, 0, 0)),
            pl.BlockSpec((1, 1, two_tile), lambda i, te, nu: (jnp.minimum(i + 1, last), 0, 0)),
            pl.BlockSpec(memory_space=pl.ANY),
            pl.BlockSpec((1, 1, d, hdim), lambda i, te, nu: (layer, te[i], 0, 0)),
            pl.BlockSpec((1, 1, d, hdim), lambda i, te, nu: (layer, te[i], 0, 0)),
            pl.BlockSpec((1, 1, hdim, d), lambda i, te, nu: (layer, te[i], 0, 0)),
        ],
        out_specs=pl.BlockSpec(memory_space=pl.ANY),
        scratch_shapes=[
            pltpu.SMEM((two_tile,), I32),
            pltpu.SMEM((two_tile,), I32),
            pltpu.VMEM((2 * tile * nchunk, LANES), PACKED),
            pltpu.VMEM((2 * tile * nchunk, LANES), PACKED),
            pltpu.VMEM((d, hdim), BF16),
            pltpu.VMEM((d, hdim), BF16),
            pltpu.VMEM((hdim, d), BF16),
            pltpu.SemaphoreType.DMA((2,)),
            pltpu.SemaphoreType.DMA((2,)),
            pltpu.SemaphoreType.DMA((2,)),
        ],
    )
    return pl.pallas_call(
        functools.partial(_moe_expert_kernel, tile=tile, d=d, t=t),
        grid_spec=grid_spec,
        out_shape=jax.ShapeDtypeStruct(((2 * t + 2 * tile) * nchunk, LANES), PACKED),
        compiler_params=_params(("arbitrary",)),
        name="moe_experts",
    )(tile_expert, n_used, slot_io, slot_io, u_rows, w_gate, w_up, w_down)


def _combine_ple_kernel(y1_ref, y2_ref, h_ref, meta_ref, p_ref, g_ref, wgate_ref, wproj_ref, fg_ref,
                        o_ref, *, tm, final_norm):
    d = h_ref.shape[1]
    meta = meta_ref[...]
    y1 = _load_token_major(y1_ref, 0, tm, d)
    y2 = _load_token_major(y2_ref, 0, tm, d)
    h2 = h_ref[...] + (meta[:, 2:3] * y1 + meta[:, 3:4] * y2)
    a = (h2 * _rms_scale(h2) * g_ref[...]).astype(BF16)
    gate = jax.nn.sigmoid(jnp.dot(a, wgate_ref[...], preferred_element_type=F32))
    proj = jnp.dot(p_ref[...].astype(BF16), wproj_ref[...], preferred_element_type=F32)
    h3 = h2 + gate * proj
    if final_norm:
        h3 = h3 * _rms_scale(h3) * fg_ref[...]
    o_ref[...] = h3


def moe_combine_ple(y_rows, h, meta, p_all, layer, ple_norm_g, ple_gate_w, ple_proj_w, final_g, final_norm,
                    tm=256):
    t, d = h.shape
    pd = p_all.shape[-1]
    nchunk = _token_rows(d)
    nt = t // tm
    return pl.pallas_call(
        functools.partial(_combine_ple_kernel, tm=tm, final_norm=final_norm),
        grid=(nt,),
        in_specs=[
            pl.BlockSpec((tm * nchunk, LANES), lambda i: (i, 0)),
            pl.BlockSpec((tm * nchunk, LANES), lambda i: (nt + i, 0)),
            pl.BlockSpec((tm, d), lambda i: (i, 0)),
            pl.BlockSpec((tm, LANES), lambda i: (i, 0)),
            pl.BlockSpec((tm, pd), lambda i: (layer * nt + i, 0)),
            pl.BlockSpec((1, d), lambda i: (0, 0)),
            pl.BlockSpec((d, d), lambda i: (0, 0)),
            pl.BlockSpec((pd, d), lambda i: (0, 0)),
            pl.BlockSpec((1, d), lambda i: (0, 0)),
        ],
        out_specs=pl.BlockSpec((tm, d), lambda i: (i, 0)),
        out_shape=jax.ShapeDtypeStruct((t, d), F32),
        compiler_params=_params(("parallel",)),
        name="moe_combine_ple",
    )(y_rows, y_rows, h, meta, p_all, ple_norm_g.reshape(1, d), ple_gate_w, ple_proj_w, final_g.reshape(1, d))


def moe_ple_layer(h, p_all, layer, ffn_norm_g, r_grp_w, r_grp_b, r_exp_w, r_exp_b, w_gate, w_up, w_down,
                  ple_norm_g, ple_gate_w, ple_proj_w, final_g, final_norm):
    t, d = h.shape
    u_rows, meta, counts = moe_router(h, ffn_norm_g, r_grp_w, r_grp_b, r_exp_w, r_exp_b)
    slot_io, tile_expert, n_used = moe_plan(meta, counts, t)
    y_rows = moe_experts(u_rows, slot_io, tile_expert, n_used, w_gate, w_up, w_down, layer, t, d)
    return moe_combine_ple(y_rows, h, meta, p_all, layer, ple_norm_g, ple_gate_w, ple_proj_w, final_g,
                           final_norm)


def _qkv_kernel(x_ref, gq_ref, gkv_ref, w_ref, cos_ref, sin_ref, o_ref, act_scr, *, nq_tiles):
    j = pl.program_id(1)

    @pl.when(j == 0)
    def _():
        x = x_ref[...]
        xn = x * _rms_scale(x)
        act_scr[0] = (xn * gq_ref[...]).astype(BF16)
        act_scr[1] = (xn * gkv_ref[...]).astype(BF16)

    a = act_scr[(j >= nq_tiles).astype(I32)]
    acc = jnp.dot(a, w_ref[...], preferred_element_type=F32)
    cos = cos_ref[0]
    sin = sin_ref[0]
    for hd in range(acc.shape[1] // HEAD_DIM):
        cols = slice(hd * HEAD_DIM, (hd + 1) * HEAD_DIM)
        seg = acc[:, cols]
        o_ref[:, cols] = (seg * cos + pltpu.roll(seg, HEAD_DIM // 2, 1) * sin).astype(o_ref.dtype)


def qkv_project(h, attn_g, kv_g, w_all, cos_t, sin_t, length, tm=512, tn=1024):
    t, d = h.shape
    n_all = w_all.shape[1]
    kvw = N_KV_HEADS * HEAD_DIM
    assert tn == kvw
    nq_tiles = (n_all - 2 * kvw) // tn
    nl = length // tm

    def table_spec():
        return pl.BlockSpec((1, tm, HEAD_DIM), lambda i, j: (jnp.clip(j - (nq_tiles - 1), 0, 2), i % nl, 0))

    return pl.pallas_call(
        functools.partial(_qkv_kernel, nq_tiles=nq_tiles),
        grid=(t // tm, n_all // tn),
        in_specs=[
            pl.BlockSpec((tm, d), lambda i, j: (i, 0)),
            pl.BlockSpec((1, d), lambda i, j: (0, 0)),
            pl.BlockSpec((1, d), lambda i, j: (0, 0)),
            pl.BlockSpec((d, tn), lambda i, j: (0, j)),
            table_spec(),
            table_spec(),
        ],
        out_specs=pl.BlockSpec((tm, tn), lambda i, j: (i, j)),
        out_shape=jax.ShapeDtypeStruct((t, n_all), BF16),
        scratch_shapes=[pltpu.VMEM((2, tm, d), BF16)],
        compiler_params=_params(("parallel", "arbitrary")),
        name="qkv_project",
    )(h, attn_g.reshape(1, d), kv_g.reshape(1, d), w_all, cos_t, sin_t)


def _attn_kernel(q0_ref, q1_ref, q2_ref, k_ref, v_ref, o_ref, q_scr, k_scr, v_scr, o_scr, lse_scr, bias_scr,
                 stage_scr, ostage_scr, *, length, patterns, unroll):
    blk = ATTN_BLOCK
    npat = len(patterns)
    q_refs = (q0_ref, q1_ref, q2_ref)
    for g in range(npat):
        for rep in range(KV_REP):
            q_scr[g * KV_REP + rep] = q_refs[g][:, rep * HEAD_DIM:(rep + 1) * HEAD_DIM].astype(F32)
    k_scr[...] = k_ref[...].astype(F32)
    v_scr[...] = v_ref[...].astype(F32)

    n_steps = patterns[0][0] // patterns[0][1]
    assert all(w // dl == n_steps for w, dl in patterns)
    qi = lax.broadcasted_iota(I32, (blk, 2 * blk), 0)
    kj = lax.broadcasted_iota(I32, (blk, 2 * blk), 1)
    for first in (0, 1):
        dist = qi - kj + first * blk
        bias_scr[first] = jnp.where((dist >= 0) & (dist <= n_steps), 0.0, NEG_INF)

    def attend(q, kk, vv, bias):
        s = lax.dot_general(q, kk, (((1,), (1,)), ((), ())), preferred_element_type=F32) + bias
        m = jnp.max(s, axis=-1, keepdims=True)
        e = jnp.exp(s - m)
        den = jnp.sum(e, axis=-1, keepdims=True)
        pv = jnp.dot(e.astype(BF16), vv, preferred_element_type=F32)
        return pv / den, jnp.broadcast_to(m + jnp.log(den), (blk, HEAD_DIM))

    for g, (window, dil) in enumerate(patterns):
        m_len = length // dil
        nblk = m_len // blk
        win = min(2 * blk, m_len)
        staged = dil > MAX_CHEAP_SUBLANE_STRIDE

        if staged:
            win = 2 * blk
            srcs = [q_scr.at[g * KV_REP + rep] for rep in range(KV_REP)] + [k_scr, v_scr]
            for a, src in enumerate(srcs):
                for res in range(dil):
                    stage_scr[a, res * m_len:(res + 1) * m_len, :] = (
                        src[pl.ds(res, m_len, stride=dil), :].astype(BF16))
                stage_scr[a, length:length + blk, :] = jnp.zeros((blk, HEAD_DIM), BF16)

        def tile(idx, c, g=g, dil=dil, nblk=nblk, win=win, m_len=m_len, staged=staged):
            r = idx // nblk
            n = idx % nblk
            ks = jnp.maximum(n - 1, 0) * blk
            bias = bias_scr[jnp.minimum(n, 1), :, 0:win]
            if staged:
                q_rows = pl.ds(pl.multiple_of(r * m_len + n * blk, blk), blk)
                k_rows = pl.ds(pl.multiple_of(r * m_len + ks, blk), win)
                kk = stage_scr[KV_REP, k_rows, :]
                vv = stage_scr[KV_REP + 1, k_rows, :]
            else:
                q_rows = pl.ds(n * (blk * dil) + r, blk, stride=dil)
                k_rows = pl.ds(ks * dil + r, win, stride=dil)
                kk = k_scr[k_rows, :].astype(BF16)
                vv = v_scr[k_rows, :].astype(BF16)
            for rep in range(KV_REP):
                slot = g * KV_REP + rep
                if staged:
                    o, lse = attend(stage_scr[rep, q_rows, :], kk, vv, bias)
                    ostage_scr[rep, q_rows, :] = o
                    ostage_scr[KV_REP + rep, q_rows, :] = lse
                else:
                    o, lse = attend(q_scr[slot, q_rows, :].astype(BF16), kk, vv, bias)
                    o_scr[slot, q_rows, :] = o
                    lse_scr[slot, q_rows, :] = lse
            return c

        lax.fori_loop(0, dil * nblk, tile, 0, unroll=unroll)

        if staged:
            for rep in range(KV_REP):
                slot = g * KV_REP + rep
                for res in range(dil):
                    rows = pl.ds(res, m_len, stride=dil)
                    o_scr[slot, rows, :] = ostage_scr[rep, res * m_len:(res + 1) * m_len, :]
                    lse_scr[slot, rows, :] = ostage_scr[KV_REP + rep, res * m_len:(res + 1) * m_len, :]

    for rep in range(KV_REP):
        lses = [lse_scr[g * KV_REP + rep] for g in range(npat)]
        top = functools.reduce(jnp.maximum, lses)
        ws = [jnp.exp(l - top) for l in lses]
        num = sum(w * o_scr[g * KV_REP + rep] for g, w in enumerate(ws))
        o_ref[:, rep * HEAD_DIM:(rep + 1) * HEAD_DIM] = (num / sum(ws)).astype(o_ref.dtype)


def dilated_attention(qkv, bsz, length):
    t = qkv.shape[0]
    qw = KV_REP * HEAD_DIM
    npat = len(DILATED_PATTERNS)
    k_base = npat * N_Q_HEADS
    v_base = k_base + N_KV_HEADS
    for _, dil in DILATED_PATTERNS:
        assert (length // dil) % ATTN_BLOCK == 0

    def q_spec(g):
        return pl.BlockSpec((length, qw), lambda b, h: (b, g * N_KV_HEADS + h))

    return pl.pallas_call(
        functools.partial(_attn_kernel, length=length, patterns=DILATED_PATTERNS, unroll=16),
        grid=(bsz, N_KV_HEADS),
        in_specs=[
            q_spec(0), q_spec(1), q_spec(2),
            pl.BlockSpec((length, HEAD_DIM), lambda b, h: (b, k_base + h)),
            pl.BlockSpec((length, HEAD_DIM), lambda b, h: (b, v_base + h)),
        ],
        out_specs=pl.BlockSpec((length, qw), lambda b, h: (b, h)),
        out_shape=jax.ShapeDtypeStruct((t, N_Q_HEADS * HEAD_DIM), BF16),
        scratch_shapes=[
            pltpu.VMEM((npat * KV_REP, length, HEAD_DIM), F32),
            pltpu.VMEM((length, HEAD_DIM), F32),
            pltpu.VMEM((length, HEAD_DIM), F32),
            pltpu.VMEM((npat * KV_REP, length, HEAD_DIM), F32),
            pltpu.VMEM((npat * KV_REP, length, HEAD_DIM), F32),
            pltpu.VMEM((2, ATTN_BLOCK, 2 * ATTN_BLOCK), F32),
            pltpu.VMEM((KV_REP + 2, length + ATTN_BLOCK, HEAD_DIM), BF16),
            pltpu.VMEM((2 * KV_REP, length, HEAD_DIM), F32),
        ],
        compiler_params=_params(("parallel", "parallel")),
        name="dilated_attention",
    )(qkv, qkv, qkv, qkv, qkv)


def _proj_residual_kernel(a_ref, w_ref, x_ref, o_ref):
    o_ref[...] = x_ref[...] + jnp.dot(a_ref[...], w_ref[...], preferred_element_type=F32)


def proj_residual(a, w, x, tm=1024, tn=512):
    t, d = x.shape
    kdim = a.shape[1]
    return pl.pallas_call(
        _proj_residual_kernel,
        grid=(t // tm, d // tn),
        in_specs=[
            pl.BlockSpec((tm, kdim), lambda i, j: (i, 0)),
            pl.BlockSpec((kdim, tn), lambda i, j: (0, j)),
            pl.BlockSpec((tm, tn), lambda i, j: (i, j)),
        ],
        out_specs=pl.BlockSpec((tm, tn), lambda i, j: (i, j)),
        out_shape=jax.ShapeDtypeStruct((t, d), F32),
        compiler_params=_params(("parallel", "arbitrary")),
        name="proj_residual",
    )(a, w, x)


def _rope_tables(length):
    inv = 1.0 / (ROPE_THETA ** (jnp.arange(0, HEAD_DIM, 2, dtype=F32) / HEAD_DIM))
    ang = jnp.arange(length, dtype=F32)[:, None] * inv[None, :]
    cos, sin = jnp.cos(ang), jnp.sin(ang)
    cos_f = jnp.concatenate([cos, cos], axis=-1)
    sin_f = jnp.concatenate([-sin, sin], axis=-1)
    scale = HEAD_DIM ** -0.5
    cos_t = jnp.stack([cos_f * scale, cos_f, jnp.ones_like(cos_f)])
    sin_t = jnp.stack([sin_f * scale, sin_f, jnp.zeros_like(sin_f)])
    return cos_t, sin_t


def kernel(x, p, s5_norm_g, s5_a_re, s5_a_im, s5_log_dt, s5_b_re, s5_b_im, s5_c_re, s5_c_im, s5_d, s5_w_glu, s5_b_glu, kv_norm_g, w_k, w_v, attn_norm_g, w_q, w_o, ffn_norm_g, router_grp_w, router_grp_b, router_exp_w, router_exp_b, w_gate, w_up, w_down, ple_norm_g, ple_gate_w, ple_proj_w, final_norm_g):
    bsz, length, d = x.shape
    depth = p.shape[0]
    n_s5 = s5_norm_g.shape[0]
    t = bsz * length
    h = x.reshape(t, d)
    cos_t, sin_t = _rope_tables(length)
    for i in range(depth):
        if i < n_s5:
            lam_re, lam_im, w_in, w_out = s5_prepare(
                s5_a_re[i], s5_a_im[i], s5_log_dt[i], s5_b_re[i], s5_b_im[i], s5_c_re[i], s5_c_im[i])
            u = rmsnorm(h, s5_norm_g[i])
            g = s5_core(u.reshape(bsz, length, d), lam_re, lam_im, w_in, w_out, s5_d[i])
            h = glu_residual(g.reshape(t, d), s5_w_glu[i].astype(BF16), s5_b_glu[i], h)
        else:
            j = i - n_s5
            assert j == 0, "shared K/V reuse across several attention layers is not implemented"
            w_all = jnp.concatenate([w_q[j], w_k, w_v], axis=1).astype(BF16)
            qkv = qkv_project(h, attn_norm_g[j], kv_norm_g, w_all, cos_t, sin_t, length)
            attn = dilated_attention(qkv, bsz, length)
            h = proj_residual(attn, w_o[j].astype(BF16), h)
        h = moe_ple_layer(
            h, p.reshape(depth * t, -1), i, ffn_norm_g[i], router_grp_w[i], router_grp_b[i], router_exp_w[i],
            router_exp_b[i], w_gate, w_up, w_down, ple_norm_g[i], ple_gate_w[i].astype(BF16),
            ple_proj_w[i].astype(BF16), final_norm_g, final_norm=(i == depth - 1))
    return h.reshape(bsz, length, d)
```
